```python
import jax, jax.numpy as jnp
from jax import lax
import numpy as np

D_MODEL = 2048
BATCH = 4
SEQ = 2048
DEPTH = 2

CTX_LEN = 256
GRID_W = 64
HEAD_DIM = 128
ATT_W = (3 * D_MODEL) // 4
N_Q_HEADS = ATT_W // HEAD_DIM
N_KV_HEADS = 4
Q_PER_KV = N_Q_HEADS // N_KV_HEADS
KV_W = N_KV_HEADS * HEAD_DIM
FNET_W = D_MODEL // 4
N_FNET_GROUPS = 4
FNET_GROUP = FNET_W // N_FNET_GROUPS
ROPE_THETA = 10000.0
ROPE_AXIS_DIM = HEAD_DIM // 2
ROPE_NFREQ = ROPE_AXIS_DIM // 2
Q_BLOCK = 128
EPS = 1e-6

OFF_Q = 0
OFF_K = OFF_Q + ATT_W
OFF_V = OFF_K + KV_W
OFF_ZA = OFF_V + KV_W
OFF_UB = OFF_ZA + ATT_W
OFF_ZB = OFF_UB + FNET_W
OFF_GA = OFF_ZB + FNET_W
OFF_GB = OFF_GA + D_MODEL
IN_W = OFF_GB + D_MODEL
R_ZA = 0
R_UB = OFF_UB - OFF_ZA
R_ZB = OFF_ZB - OFF_ZA
R_GA = OFF_GA - OFF_ZA
R_GB = OFF_GB - OFF_ZA

kernel_name = "hybrid_gqa_fnet_gated_dit_block"


def _rmsnorm(x, g):
    xf = x.astype(jnp.float32)
    y = xf * lax.rsqrt(jnp.mean(xf * xf, axis=-1, keepdims=True) + EPS)
    return (y * g.astype(jnp.float32)).astype(x.dtype)


def _modulate(h, shift, scale):
    return h * (1 + scale) + shift


def _heads(p, lo, n_heads):
    b, n, _ = p.shape
    return p[..., lo:lo + n_heads * HEAD_DIM].reshape(b, n, n_heads, HEAD_DIM)


def _axial_rope_tables(n_tokens):
    rows = n_tokens // GRID_W
    row_pos = jnp.repeat(jnp.arange(rows, dtype=jnp.float32), GRID_W)
    col_pos = jnp.tile(jnp.arange(GRID_W, dtype=jnp.float32), rows)
    inv = ROPE_THETA ** (-jnp.arange(ROPE_NFREQ, dtype=jnp.float32) / ROPE_NFREQ)
    ang = jnp.stack([row_pos[:, None] * inv, col_pos[:, None] * inv], axis=1)
    return jnp.cos(ang), jnp.sin(ang)


def _rope(x, cos, sin):
    b, n, h, _ = x.shape
    xf = x.astype(jnp.float32).reshape(b, n, h, 2, 2, ROPE_NFREQ)
    x1, x2 = xf[..., 0, :], xf[..., 1, :]
    cc, ss = cos[None, :, None], sin[None, :, None]
    out = jnp.stack([x1 * cc - x2 * ss, x2 * cc + x1 * ss], axis=-2)
    return out.reshape(b, n, h, HEAD_DIM).astype(x.dtype)


def _attention(q, k, v):
    b, n = q.shape[:2]
    nblk = n // Q_BLOCK
    qb = q.reshape(b, nblk, Q_BLOCK, N_KV_HEADS, Q_PER_KV, HEAD_DIM).transpose(1, 0, 2, 3, 4, 5)

    def block(qi):
        s = jnp.einsum('bqhgd,bkhd->bhgqk', qi, k, preferred_element_type=jnp.float32)
        w = jax.nn.softmax(s, axis=-1).astype(v.dtype)
        return jnp.einsum('bhgqk,bkhd->bqhgd', w, v)

    o = lax.map(block, qb)
    return o.transpose(1, 0, 2, 3, 4, 5).reshape(b, n, ATT_W)


def _fourier_mix(u):
    b, n, _ = u.shape
    ug = u.astype(jnp.float32).reshape(b, n, N_FNET_GROUPS, FNET_GROUP)
    y = jnp.fft.fftn(ug, axes=(1, 3), norm="ortho").real
    return y.reshape(b, n, FNET_W).astype(u.dtype)


def _merge(p_rest, attn, w_pa, w_pb, w_o):
    z_a = p_rest[..., R_ZA:R_UB]
    u_b = p_rest[..., R_UB:R_ZB]
    z_b = p_rest[..., R_ZB:R_GA]
    g_a = jax.nn.sigmoid(p_rest[..., R_GA:R_GB])
    g_b = jax.nn.sigmoid(p_rest[..., R_GB:])
    y_a = (attn * jax.nn.silu(z_a)) @ w_pa
    y_b = (_fourier_mix(u_b) * jax.nn.silu(z_b)) @ w_pb
    return (g_a * y_a + g_b * y_b) @ w_o


def setup_inputs(seed: int = 0) -> dict:
    key = jax.random.key(seed)
    ks = jax.random.split(key, 16)
    f32 = jnp.float32
    nrm = lambda k, shape, s: jax.random.normal(k, shape, f32) * s
    return {
        "x": nrm(ks[0], (BATCH, SEQ, D_MODEL), 1.0),
        "c": nrm(ks[1], (BATCH, D_MODEL), 1.0),
        "ctx": nrm(ks[2], (BATCH, CTX_LEN, D_MODEL), 1.0),
        "c_ctx": nrm(ks[3], (D_MODEL,), 1.0),
        "w_ada": nrm(ks[4], (DEPTH, D_MODEL, 3 * D_MODEL), D_MODEL ** -0.5),
        "b_ada": nrm(ks[5], (DEPTH, 3 * D_MODEL), 0.01),
        "norm_g": 1.0 + nrm(ks[6], (DEPTH, D_MODEL), 0.01),
        "w_in": nrm(ks[7], (DEPTH, D_MODEL, IN_W), D_MODEL ** -0.5),
        "q_norm_g": 1.0 + nrm(ks[8], (DEPTH, HEAD_DIM), 0.01),
        "k_norm_g": 1.0 + nrm(ks[9], (DEPTH, HEAD_DIM), 0.01),
        "w_proj_a": nrm(ks[10], (DEPTH, ATT_W, D_MODEL), ATT_W ** -0.5),
        "w_proj_b": nrm(ks[11], (DEPTH, FNET_W, D_MODEL), FNET_W ** -0.5),
        "w_out": nrm(ks[12], (DEPTH, D_MODEL, D_MODEL), D_MODEL ** -0.5),
        "final_g": 1.0 + nrm(ks[13], (D_MODEL,), 0.01),
    }


def reference(x, c, ctx, c_ctx, w_ada, b_ada, norm_g, w_in, q_norm_g, k_norm_g,
              w_proj_a, w_proj_b, w_out, final_g):
    n_lat = x.shape[1]
    cos, sin = _axial_rope_tables(n_lat)
    q_scale = HEAD_DIM ** -0.5
    silu_c = jax.nn.silu(c)
    silu_cc = jax.nn.silu(c_ctx)
    xs, cs = x, ctx
    for l in range(DEPTH):
        last = l == DEPTH - 1
        w_l = w_in[l]
        shift, scale, gate = jnp.split(silu_c @ w_ada[l] + b_ada[l], 3, axis=-1)
        shift_c, scale_c, gate_c = jnp.split(silu_cc @ w_ada[l] + b_ada[l], 3, axis=-1)
        h = _modulate(_rmsnorm(xs, norm_g[l]), shift[:, None], scale[:, None])
        hc = _modulate(_rmsnorm(cs, norm_g[l]), shift_c, scale_c)

        pc_kv = hc @ w_l[:, OFF_K:OFF_ZA]
        kc = _rmsnorm(_heads(pc_kv, 0, N_KV_HEADS), k_norm_g[l])
        vc = _heads(pc_kv, KV_W, N_KV_HEADS)

        p = h @ w_l
        q = _rope(_rmsnorm(_heads(p, OFF_Q, N_Q_HEADS), q_norm_g[l]), cos, sin) * q_scale
        k = _rope(_rmsnorm(_heads(p, OFF_K, N_KV_HEADS), k_norm_g[l]), cos, sin)
        v = _heads(p, OFF_V, N_KV_HEADS)
        attn = _attention(q, jnp.concatenate([k, kc], axis=1), jnp.concatenate([v, vc], axis=1))
        y = _merge(p[..., OFF_ZA:], attn, w_proj_a[l], w_proj_b[l], w_out[l])
        new_xs = xs + gate[:, None] * y

        if not last:
            qc = _rmsnorm(_heads(hc @ w_l[:, OFF_Q:OFF_K], 0, N_Q_HEADS), q_norm_g[l]) * q_scale
            attn_c = _attention(qc, kc, vc)
            yc = _merge(hc @ w_l[:, OFF_ZA:], attn_c, w_proj_a[l], w_proj_b[l], w_out[l])
            cs = cs + gate_c * yc
        xs = new_xs
    return _rmsnorm(xs, final_g)
```

```python
import functools

import numpy as np
import jax
import jax.numpy as jnp
from jax import lax
from jax.experimental import pallas as pl
from jax.experimental.pallas import tpu as pltpu

F32 = jnp.float32
BF16 = jnp.bfloat16

D_MODEL = 2048
DEPTH = 2
CTX_LEN = 256
GRID_W = 64
HEAD_DIM = 128
ATT_W = (3 * D_MODEL) // 4
N_Q_HEADS = ATT_W // HEAD_DIM
N_KV_HEADS = 4
Q_PER_KV = N_Q_HEADS // N_KV_HEADS
KV_W = N_KV_HEADS * HEAD_DIM
FNET_W = D_MODEL // 4
N_FNET_GROUPS = 4
FNET_GROUP = FNET_W // N_FNET_GROUPS
ROPE_THETA = 10000.0
ROPE_NFREQ = HEAD_DIM // 4
EPS = 1e-6
IN_W = 2 * ATT_W + 2 * KV_W + 2 * FNET_W + 2 * D_MODEL

V7X_VMEM_BYTES = 64 * 1024 * 1024
VMEM_LIMIT = V7X_VMEM_BYTES - 8 * 1024 * 1024

COL_TILE = 512
REGIONS = (
    ("q", 0, 3), ("k", 3, 4), ("v", 4, 5), ("za", 5, 8),
    ("ub", 8, 9), ("zb", 9, 10), ("ga", 10, 14), ("gb", 14, 18),
)
N_COL_TILES = IN_W // COL_TILE
MOD_ROWS = 8


def _silu(x):
    return x * jax.nn.sigmoid(x)


def _ada_kernel(cv_ref, w_ref, b_ref, o_ref):
    s = _silu(cv_ref[...]).astype(BF16)
    o_ref[0] = jnp.dot(s, w_ref[0].astype(BF16), preferred_element_type=F32) + b_ref[0]


def _ada(cvec, w_ada, b_ada):
    tn = 1024
    n = 3 * D_MODEL
    return pl.pallas_call(
        _ada_kernel,
        grid=(DEPTH, n // tn),
        in_specs=[
            pl.BlockSpec((MOD_ROWS, D_MODEL), lambda l, j: (0, 0)),
            pl.BlockSpec((1, D_MODEL, tn), lambda l, j: (l, 0, j)),
            pl.BlockSpec((1, 1, tn), lambda l, j: (l, 0, j)),
        ],
        out_specs=pl.BlockSpec((1, MOD_ROWS, tn), lambda l, j: (l, 0, j)),
        out_shape=jax.ShapeDtypeStruct((DEPTH, MOD_ROWS, n), F32),
        compiler_params=pltpu.CompilerParams(
            dimension_semantics=("arbitrary", "arbitrary"), vmem_limit_bytes=VMEM_LIMIT),
        name="ada",
    )(cvec, w_ada, b_ada.reshape(DEPTH, 1, n))


def _head_norm_rope(acc, g, c, s1, s2, out_scale):
    outs = []
    for hh in range(COL_TILE // HEAD_DIM):
        xh = acc[:, hh * HEAD_DIM:(hh + 1) * HEAD_DIM]
        ms = jnp.mean(xh * xh, axis=-1, keepdims=True)
        y = xh * lax.rsqrt(ms + EPS) * g
        r = (y * c + pltpu.roll(y, HEAD_DIM - ROPE_NFREQ, 1) * s1
             + pltpu.roll(y, ROPE_NFREQ, 1) * s2)
        outs.append(r * out_scale)
    return jnp.concatenate(outs, axis=-1)


def _in_proj_kernel(regions, x_ref, shift_ref, scale_ref, g_ref, w_ref, qg_ref, kg_ref,
                    c_ref, s1_ref, s2_ref, *rest):
    out_refs, h_ref = rest[:-1], rest[-1]
    j = pl.program_id(1)

    @pl.when(j == 0)
    def _():
        xf = x_ref[...]
        ms = jnp.mean(xf * xf, axis=-1, keepdims=True)
        y = xf * lax.rsqrt(ms + EPS) * g_ref[...]
        h_ref[...] = (y * (1.0 + scale_ref[0]) + shift_ref[0]).astype(BF16)

    acc = jnp.dot(h_ref[...], w_ref[...], preferred_element_type=F32)

    for (name, a, b), o_ref in zip(regions, out_refs):

        @pl.when((j >= a) & (j < b))
        def _(name=name, o_ref=o_ref):
            if name == "q":
                o_ref[...] = _head_norm_rope(
                    acc, qg_ref[...], c_ref[...], s1_ref[...], s2_ref[...],
                    HEAD_DIM ** -0.5).astype(BF16)
            elif name == "k":
                o_ref[...] = _head_norm_rope(
                    acc, kg_ref[...], c_ref[...], s1_ref[...], s2_ref[...], 1.0).astype(BF16)
            else:
                o_ref[...] = acc.astype(BF16)


def _in_proj(x2d, mod, mod_row_of_tile, norm_g, w, q_g, k_g, tables, table_tile_of_tile,
             tile_lo, tile_hi, tm):
    t = x2d.shape[0]
    regions = tuple((n, max(a, tile_lo) - tile_lo, min(b, tile_hi) - tile_lo)
                    for n, a, b in REGIONS if max(a, tile_lo) < min(b, tile_hi))
    c_tab, s1_tab, s2_tab = tables

    def out_map(a, b):
        return lambda i, j: (i, jnp.clip(j - a, 0, b - a - 1))

    tab_spec = pl.BlockSpec((tm, HEAD_DIM), lambda i, j: (table_tile_of_tile(i), 0))
    outs = pl.pallas_call(
        functools.partial(_in_proj_kernel, regions),
        grid=(t // tm, tile_hi - tile_lo),
        in_specs=[
            pl.BlockSpec((tm, D_MODEL), lambda i, j: (i, 0)),
            pl.BlockSpec((1, 1, D_MODEL), lambda i, j: (mod_row_of_tile(i) * 3, 0, 0)),
            pl.BlockSpec((1, 1, D_MODEL), lambda i, j: (mod_row_of_tile(i) * 3 + 1, 0, 0)),
            pl.BlockSpec((1, D_MODEL), lambda i, j: (0, 0)),
            pl.BlockSpec((D_MODEL, COL_TILE), lambda i, j: (0, j + tile_lo)),
            pl.BlockSpec((1, HEAD_DIM), lambda i, j: (0, 0)),
            pl.BlockSpec((1, HEAD_DIM), lambda i, j: (0, 0)),
            tab_spec, tab_spec, tab_spec,
        ],
        out_specs=[pl.BlockSpec((tm, COL_TILE), out_map(a, b)) for _, a, b in regions],
        out_shape=[jax.ShapeDtypeStruct((t, (b - a) * COL_TILE), BF16) for _, a, b in regions],
        scratch_shapes=[pltpu.VMEM((tm, D_MODEL), BF16)],
        compiler_params=pltpu.CompilerParams(
            dimension_semantics=("arbitrary", "arbitrary"), vmem_limit_bytes=VMEM_LIMIT),
        name="in_proj",
    )(x2d, mod, mod, norm_g, w, q_g, k_g, c_tab, s1_tab, s2_tab)
    return {n: o for (n, _, _), o in zip(regions, outs)}


def _attn_kernel(n_sets, tq, q_ref, *rest):
    kv_refs, o_ref = rest[:2 * n_sets], rest[2 * n_sets]
    q3 = q_ref[...]
    qs = jnp.concatenate(
        [q3[:, i * HEAD_DIM:(i + 1) * HEAD_DIM] for i in range(Q_PER_KV)], axis=0)
    scores = [
        lax.dot_general(qs, kv_refs[2 * t][...], (((1,), (1,)), ((), ())),
                        preferred_element_type=F32)
        for t in range(n_sets)
    ]
    m = functools.reduce(jnp.maximum, [s.max(axis=-1, keepdims=True) for s in scores])
    probs = [jnp.exp(s - m) for s in scores]
    denom = functools.reduce(jnp.add, [p.sum(axis=-1, keepdims=True) for p in probs])
    o = functools.reduce(jnp.add, [
        jnp.dot(p.astype(BF16), kv_refs[2 * t + 1][...], preferred_element_type=F32)
        for t, p in enumerate(probs)
    ])
    o = o / denom
    for i in range(Q_PER_KV):
        o_ref[:, i * HEAD_DIM:(i + 1) * HEAD_DIM] = o[i * tq:(i + 1) * tq].astype(BF16)


def _attention(q, kv_sets, batch, tq):
    t = q.shape[0]
    nq = t // batch // tq
    gw = Q_PER_KV * HEAD_DIM
    in_specs = [pl.BlockSpec((tq, gw), lambda b, g, i: (b * nq + i, g))]
    args = [q]
    for k, v in kv_sets:
        m_t = k.shape[0] // batch
        spec = pl.BlockSpec((m_t, HEAD_DIM), lambda b, g, i: (b, g))
        in_specs += [spec, spec]
        args += [k, v]
    return pl.pallas_call(
        functools.partial(_attn_kernel, len(kv_sets), tq),
        grid=(batch, N_KV_HEADS, nq),
        in_specs=in_specs,
        out_specs=pl.BlockSpec((tq, gw), lambda b, g, i: (b * nq + i, g)),
        out_shape=jax.ShapeDtypeStruct((t, ATT_W), BF16),
        compiler_params=pltpu.CompilerParams(
            dimension_semantics=("arbitrary", "arbitrary", "arbitrary"),
            vmem_limit_bytes=VMEM_LIMIT),
        name="attn",
    )(*args)


def _dft_tables(n):
    def cs(m):
        idx = np.arange(m, dtype=np.int64)
        ang = 2.0 * np.pi * ((idx[:, None] * idx[None, :]) % m).astype(np.float64) / m
        return np.cos(ang) / np.sqrt(m), np.sin(ang) / np.sqrt(m)

    cn, sn = cs(n)
    cc, sc = cs(FNET_GROUP)
    return (np.concatenate([cn, -sn], axis=1).astype(np.float32),
            cc.astype(np.float32), sc.astype(np.float32))


def _fourier_kernel(n, u_ref, w_ref, cc_ref, sc_ref, o_ref, x_ref):
    @pl.when(pl.program_id(1) == 0)
    def _():
        for g in range(N_FNET_GROUPS):
            cols = slice(g * FNET_GROUP, (g + 1) * FNET_GROUP)
            ug = u_ref[:, cols]
            x_ref[0:n, cols] = jnp.dot(ug, cc_ref[...], preferred_element_type=F32).astype(BF16)
            x_ref[n:2 * n, cols] = jnp.dot(ug, sc_ref[...], preferred_element_type=F32).astype(BF16)

    o_ref[...] = jnp.dot(w_ref[...], x_ref[...], preferred_element_type=F32).astype(BF16)


def _fourier(u, batch, tr):
    n = u.shape[0] // batch
    w, cc, sc = (jnp.asarray(a).astype(BF16) for a in _dft_tables(n))
    nr = n // tr
    return pl.pallas_call(
        functools.partial(_fourier_kernel, n),
        grid=(batch, nr),
        in_specs=[
            pl.BlockSpec((n, FNET_W), lambda b, r: (b, 0)),
            pl.BlockSpec((tr, 2 * n), lambda b, r: (r, 0)),
            pl.BlockSpec((FNET_GROUP, FNET_GROUP), lambda b, r: (0, 0)),
            pl.BlockSpec((FNET_GROUP, FNET_GROUP), lambda b, r: (0, 0)),
        ],
        out_specs=pl.BlockSpec((tr, FNET_W), lambda b, r: (b * nr + r, 0)),
        out_shape=jax.ShapeDtypeStruct(u.shape, BF16),
        scratch_shapes=[pltpu.VMEM((2 * n, FNET_W), BF16)],
        compiler_params=pltpu.CompilerParams(
            dimension_semantics=("arbitrary", "arbitrary"), vmem_limit_bytes=VMEM_LIMIT),
        name="fourier",
    )(u, w, cc, sc)


def _merge_kernel(final, attn_ref, za_ref, fm_ref, zb_ref, ga_ref, gb_ref, x_ref, gate_ref,
                  wpa_ref, wpb_ref, wo_ref, fg_ref, o_ref):
    ta = (attn_ref[...].astype(F32) * _silu(za_ref[...].astype(F32))).astype(BF16)
    ya = jnp.dot(ta, wpa_ref[...], preferred_element_type=F32)
    tb = (fm_ref[...].astype(F32) * _silu(zb_ref[...].astype(F32))).astype(BF16)
    yb = jnp.dot(tb, wpb_ref[...], preferred_element_type=F32)
    mix = (jax.nn.sigmoid(ga_ref[...].astype(F32)) * ya
           + jax.nn.sigmoid(gb_ref[...].astype(F32)) * yb).astype(BF16)
    y = jnp.dot(mix, wo_ref[...], preferred_element_type=F32)
    out = x_ref[...] + gate_ref[0] * y
    if final:
        ms = jnp.mean(out * out, axis=-1, keepdims=True)
        out = out * lax.rsqrt(ms + EPS) * fg_ref[...]
    o_ref[...] = out


def _merge(p, attn, fm, x2d, mod, mod_row_of_tile, w_pa, w_pb, w_o, final_g, final, tm):
    t = x2d.shape[0]

    def rows(width):
        return pl.BlockSpec((tm, width), lambda i: (i, 0))

    def whole(shape):
        return pl.BlockSpec(shape, lambda i: (0, 0), pipeline_mode=pl.Buffered(1))

    return pl.pallas_call(
        functools.partial(_merge_kernel, final),
        grid=(t // tm,),
        in_specs=[
            rows(ATT_W), rows(ATT_W), rows(FNET_W), rows(FNET_W), rows(D_MODEL), rows(D_MODEL),
            rows(D_MODEL),
            pl.BlockSpec((1, 1, D_MODEL), lambda i: (mod_row_of_tile(i) * 3 + 2, 0, 0)),
            whole((ATT_W, D_MODEL)), whole((FNET_W, D_MODEL)), whole((D_MODEL, D_MODEL)),
            pl.BlockSpec((1, D_MODEL), lambda i: (0, 0)),
        ],
        out_specs=rows(D_MODEL),
        out_shape=jax.ShapeDtypeStruct((t, D_MODEL), F32),
        compiler_params=pltpu.CompilerParams(
            dimension_semantics=("arbitrary",), vmem_limit_bytes=VMEM_LIMIT),
        name="merge",
    )(attn, p["za"], fm, p["zb"], p["ga"], p["gb"], x2d, mod, w_pa, w_pb, w_o, final_g)


def _rope_tables(n_tokens):
    t = np.arange(n_tokens)
    pos = np.stack([t // GRID_W, t % GRID_W], axis=1).astype(np.float32)
    inv = (np.float32(ROPE_THETA) ** (-np.arange(ROPE_NFREQ, dtype=np.float32) / ROPE_NFREQ))
    ang = (pos[:, :, None] * inv[None, None, :]).astype(np.float32).astype(np.float64)
    cos, sin = np.cos(ang), np.sin(ang)
    zero = np.zeros_like(sin)
    c = np.stack([cos, cos], axis=2).reshape(n_tokens, HEAD_DIM)
    s1 = np.stack([-sin, zero], axis=2).reshape(n_tokens, HEAD_DIM)
    s2 = np.stack([zero, sin], axis=2).reshape(n_tokens, HEAD_DIM)
    return tuple(jnp.asarray(a.astype(np.float32)) for a in (c, s1, s2))


def _identity_tables(n_tokens):
    one = jnp.ones((n_tokens, HEAD_DIM), F32)
    zero = jnp.zeros((n_tokens, HEAD_DIM), F32)
    return one, zero, zero


def kernel(x, c, ctx, c_ctx, w_ada, b_ada, norm_g, w_in, q_norm_g, k_norm_g,
           w_proj_a, w_proj_b, w_out, final_g):
    batch, n_lat, d = x.shape
    n_ctx = ctx.shape[1]
    tm_lat, tm_ctx, tm_merge = 1024, 512, 256
    ctx_row = batch

    cvec = jnp.concatenate(
        [c, c_ctx[None], jnp.zeros((MOD_ROWS - batch - 1, d), F32)], axis=0)
    mods = _ada(cvec, w_ada, b_ada)

    rope = _rope_tables(n_lat)
    no_rope = _identity_tables(tm_ctx)
    lat_tiles = n_lat // tm_lat
    merge_tiles = n_lat // tm_merge

    xs = x.reshape(batch * n_lat, d)
    cs = ctx.reshape(batch * n_ctx, d)
    for l in range(DEPTH):
        last = l == DEPTH - 1
        mod = mods[l].reshape(MOD_ROWS * 3, 1, d)
        w_l = w_in[l].astype(BF16)
        w_pa, w_pb, w_o = (w[l].astype(BF16) for w in (w_proj_a, w_proj_b, w_out))
        g_l = norm_g[l].reshape(1, d)
        qg, kg = q_norm_g[l].reshape(1, HEAD_DIM), k_norm_g[l].reshape(1, HEAD_DIM)
        fg = final_g.reshape(1, d)

        lo, hi = (REGIONS[1][1], REGIONS[2][2]) if last else (0, N_COL_TILES)
        pc = _in_proj(cs, mod, lambda i: ctx_row, g_l, w_l, qg, kg, no_rope, lambda i: 0,
                      lo, hi, tm_ctx)
        p = _in_proj(xs, mod, lambda i: i // lat_tiles, g_l, w_l, qg, kg, rope,
                     lambda i: i % lat_tiles, 0, N_COL_TILES, tm_lat)

        attn = _attention(p["q"], [(p["k"], p["v"]), (pc["k"], pc["v"])], batch, 256)
        fm = _fourier(p["ub"], batch, 512)
        new_xs = _merge(p, attn, fm, xs, mod, lambda i: i // merge_tiles, w_pa, w_pb, w_o,
                        fg, last, tm_merge)
        if not last:
            attn_c = _attention(pc["q"], [(pc["k"], pc["v"])], batch, n_ctx)
            fm_c = _fourier(pc["ub"], batch, n_ctx)
            cs = _merge(pc, attn_c, fm_c, cs, mod, lambda i: ctx_row, w_pa, w_pb, w_o,
                        fg, False, tm_merge)
        xs = new_xs
    return xs.reshape(batch, n_lat, d)
```

```python
import functools
import math

import numpy as np
import jax
import jax.numpy as jnp
from jax import lax
from jax.experimental import pallas as pl
from jax.experimental.pallas import tpu as pltpu

F32 = jnp.float32
BF16 = jnp.bfloat16

D_MODEL = 2048
DEPTH = 2
CTX_LEN = 256
GRID_W = 64
HEAD_DIM = 128
ATT_W = (3 * D_MODEL) // 4
N_Q_HEADS = ATT_W // HEAD_DIM
N_KV_HEADS = 4
Q_PER_KV = N_Q_HEADS // N_KV_HEADS
KV_W = N_KV_HEADS * HEAD_DIM
FNET_W = D_MODEL // 4
N_FNET_GROUPS = 4
FNET_GROUP = FNET_W // N_FNET_GROUPS
ROPE_THETA = 10000.0
ROPE_NFREQ = HEAD_DIM // 4
EPS = 1e-6
IN_W = 2 * ATT_W + 2 * KV_W + 2 * FNET_W + 2 * D_MODEL
Q_SCALE = HEAD_DIM ** -0.5 * math.log2(math.e)

V7X_VMEM_BYTES = 64 * 1024 * 1024
VMEM_LIMIT = V7X_VMEM_BYTES - 8 * 1024 * 1024

COL_TILE = 512
N_COL_TILES = IN_W // COL_TILE
K_TILE = ATT_W // COL_TILE
V_TILE = K_TILE + 1
ZB_TILE = (2 * ATT_W + 2 * KV_W + FNET_W) // COL_TILE
REST_W = IN_W - ATT_W - 2 * KV_W
REST_ZA_BLOCK = 0
REST_UB_BLOCK = ATT_W // FNET_W
REST_GA_BLOCK = 1
REST_GB_BLOCK = 2
REST_ZB_BLOCK = REST_W // FNET_W - 1
MOD_ROWS = 8


def _silu(x):
    return x * jax.nn.sigmoid(x)


def _ada_kernel(cv_ref, w_ref, b_ref, o_ref):
    s = _silu(cv_ref[...]).astype(BF16)
    o_ref[0] = jnp.dot(s, w_ref[0].astype(BF16), preferred_element_type=F32) + b_ref[0]


def _ada(cvec, w_ada, b_ada):
    tn = 1024
    n = 3 * D_MODEL
    return pl.pallas_call(
        _ada_kernel,
        grid=(DEPTH, n // tn),
        in_specs=[
            pl.BlockSpec((MOD_ROWS, D_MODEL), lambda l, j: (0, 0)),
            pl.BlockSpec((1, D_MODEL, tn), lambda l, j: (l, 0, j)),
            pl.BlockSpec((1, 1, tn), lambda l, j: (l, 0, j)),
        ],
        out_specs=pl.BlockSpec((1, MOD_ROWS, tn), lambda l, j: (l, 0, j)),
        out_shape=jax.ShapeDtypeStruct((DEPTH, MOD_ROWS, n), F32),
        compiler_params=pltpu.CompilerParams(
            dimension_semantics=("arbitrary", "arbitrary"), vmem_limit_bytes=VMEM_LIMIT),
        name="ada",
    )(cvec, w_ada, b_ada.reshape(DEPTH, 1, n))


def _weight_tile(step):
    return jnp.where(step < ZB_TILE, step, jnp.where(step < N_COL_TILES - 1, step + 1, ZB_TILE))


def _split_bf16(x):
    hi = x.astype(BF16)
    return hi, (x - hi.astype(F32)).astype(BF16)


def _head_norm_rope(acc, head_mean, g, c, s1, s2):
    sq_hi, sq_lo = _split_bf16(acc * acc)
    ms = (jnp.dot(sq_hi, head_mean, preferred_element_type=F32)
          + jnp.dot(sq_lo, head_mean, preferred_element_type=F32))
    y = acc * lax.rsqrt(ms + EPS) * g
    outs = []
    for hh in range(COL_TILE // HEAD_DIM):
        yh = y[:, hh * HEAD_DIM:(hh + 1) * HEAD_DIM]
        outs.append(yh * c + pltpu.roll(yh, HEAD_DIM - ROPE_NFREQ, 1) * s1
                    + pltpu.roll(yh, ROPE_NFREQ, 1) * s2)
    return jnp.concatenate(outs, axis=-1)


def _in_proj_kernel(step_lo, names, x_ref, shift_ref, scale_ref, g_ref, w_ref, qg_ref, kg_ref,
                    hm_ref, c_ref, s1_ref, s2_ref, *rest):
    out = dict(zip(names, rest[:-1]))
    h_ref = rest[-1]
    step = pl.program_id(1) + step_lo

    @pl.when(pl.program_id(1) == 0)
    def _():
        xf = x_ref[...]
        ms = jnp.mean(xf * xf, axis=-1, keepdims=True)
        y = xf * lax.rsqrt(ms + EPS) * g_ref[...]
        h_ref[...] = (y * (1.0 + scale_ref[0]) + shift_ref[0]).astype(BF16)

    def acc():
        return jnp.dot(h_ref[...], w_ref[...], preferred_element_type=F32)

    if "q" in out:
        @pl.when(step < K_TILE)
        def _():
            out["q"][...] = _head_norm_rope(
                acc(), hm_ref[...], qg_ref[...], c_ref[...], s1_ref[...], s2_ref[...]).astype(BF16)

    if "k" in out:
        @pl.when(step == K_TILE)
        def _():
            out["k"][...] = _head_norm_rope(
                acc(), hm_ref[...], kg_ref[...], c_ref[...], s1_ref[...], s2_ref[...]).astype(BF16)

    if "vt" in out:
        @pl.when(step == V_TILE)
        def _():
            out["vt"][...] = acc().T.astype(BF16)

    if "rest" in out:
        @pl.when(step > V_TILE)
        def _():
            out["rest"][...] = acc().astype(BF16)


def _in_proj(x2d, mod, mod_row_of_tile, norm_g, w, q_g, k_g, tables, table_tile_of_tile,
             step_lo, step_hi, tm):
    t = x2d.shape[0]
    c_tab, s1_tab, s2_tab = tables
    n_rest = REST_W // COL_TILE
    heads_per_tile = COL_TILE // HEAD_DIM
    head_mean = jnp.asarray(
        np.kron(np.eye(heads_per_tile), np.full((HEAD_DIM, HEAD_DIM), 1.0 / HEAD_DIM)), BF16)
    specs = {
        "q": (pl.BlockSpec((tm, COL_TILE), lambda i, j: (i, jnp.clip(j + step_lo, 0, K_TILE - 1))),
              jax.ShapeDtypeStruct((t, ATT_W), BF16), step_lo < K_TILE),
        "k": (pl.BlockSpec((tm, COL_TILE), lambda i, j: (i, 0)),
              jax.ShapeDtypeStruct((t, KV_W), BF16), step_lo <= K_TILE < step_hi),
        "vt": (pl.BlockSpec((COL_TILE, tm), lambda i, j: (0, i)),
               jax.ShapeDtypeStruct((KV_W, t), BF16), step_lo <= V_TILE < step_hi),
        "rest": (pl.BlockSpec((tm, COL_TILE),
                              lambda i, j: (i, jnp.clip(j + step_lo - V_TILE - 1, 0, n_rest - 1))),
                 jax.ShapeDtypeStruct((t, REST_W), BF16), step_hi > V_TILE + 1),
    }
    names = tuple(n for n, (_, _, present) in specs.items() if present)
    tab_spec = pl.BlockSpec((tm, HEAD_DIM), lambda i, j: (table_tile_of_tile(i), 0))
    outs = pl.pallas_call(
        functools.partial(_in_proj_kernel, step_lo, names),
        grid=(t // tm, step_hi - step_lo),
        in_specs=[
            pl.BlockSpec((tm, D_MODEL), lambda i, j: (i, 0)),
            pl.BlockSpec((1, 1, D_MODEL), lambda i, j: (mod_row_of_tile(i) * 3, 0, 0)),
            pl.BlockSpec((1, 1, D_MODEL), lambda i, j: (mod_row_of_tile(i) * 3 + 1, 0, 0)),
            pl.BlockSpec((1, D_MODEL), lambda i, j: (0, 0)),
            pl.BlockSpec((D_MODEL, COL_TILE), lambda i, j: (0, _weight_tile(j + step_lo))),
            pl.BlockSpec((1, COL_TILE), lambda i, j: (0, 0)),
            pl.BlockSpec((1, COL_TILE), lambda i, j: (0, 0)),
            pl.BlockSpec((COL_TILE, COL_TILE), lambda i, j: (0, 0)),
            tab_spec, tab_spec, tab_spec,
        ],
        out_specs=[specs[n][0] for n in names],
        out_shape=[specs[n][1] for n in names],
        scratch_shapes=[pltpu.VMEM((tm, D_MODEL), BF16)],
        compiler_params=pltpu.CompilerParams(
            dimension_semantics=("arbitrary", "arbitrary"), vmem_limit_bytes=VMEM_LIMIT),
        name="in_proj",
    )(x2d, mod, mod, norm_g, w, q_g, k_g, head_mean, c_tab, s1_tab, s2_tab)
    return dict(zip(names, outs))


def _attn_kernel(n_sets, tq, q_ref, *rest):
    kv_refs, o_ref = rest[:2 * n_sets], rest[2 * n_sets]
    q3 = q_ref[...]
    qs = jnp.concatenate(
        [q3[:, i * HEAD_DIM:(i + 1) * HEAD_DIM] for i in range(Q_PER_KV)], axis=0)
    scores = [
        lax.dot_general(kv_refs[2 * t][...], qs, (((1,), (1,)), ((), ())),
                        preferred_element_type=F32)
        for t in range(n_sets)
    ]
    m = functools.reduce(jnp.maximum, [s.max(axis=0, keepdims=True) for s in scores])
    probs = [jnp.exp2(s - m) for s in scores]
    denom = functools.reduce(jnp.add, [p.sum(axis=0, keepdims=True) for p in probs])
    o_t = functools.reduce(jnp.add, [
        jnp.dot(kv_refs[2 * t + 1][...], p.astype(BF16), preferred_element_type=F32)
        for t, p in enumerate(probs)
    ])
    o_t = o_t / denom
    for i in range(Q_PER_KV):
        o_ref[:, i * HEAD_DIM:(i + 1) * HEAD_DIM] = o_t[:, i * tq:(i + 1) * tq].T.astype(BF16)


def _attention(q, kv_sets, batch, tq):
    t = q.shape[0]
    nq = t // batch // tq
    gw = Q_PER_KV * HEAD_DIM
    in_specs = [pl.BlockSpec((tq, gw), lambda b, g, i: (b * nq + i, g))]
    args = [q]
    for k, vt in kv_sets:
        m_t = k.shape[0] // batch
        in_specs += [pl.BlockSpec((m_t, HEAD_DIM), lambda b, g, i: (b, g)),
                     pl.BlockSpec((HEAD_DIM, m_t), lambda b, g, i: (g, b))]
        args += [k, vt]
    return pl.pallas_call(
        functools.partial(_attn_kernel, len(kv_sets), tq),
        grid=(batch, N_KV_HEADS, nq),
        in_specs=in_specs,
        out_specs=pl.BlockSpec((tq, gw), lambda b, g, i: (b * nq + i, g)),
        out_shape=jax.ShapeDtypeStruct((t, ATT_W), BF16),
        compiler_params=pltpu.CompilerParams(
            dimension_semantics=("arbitrary", "arbitrary", "arbitrary"),
            vmem_limit_bytes=VMEM_LIMIT),
        name="attn",
    )(*args)


def _dft_tables(n):
    def cs(m):
        idx = np.arange(m, dtype=np.int64)
        ang = 2.0 * np.pi * ((idx[:, None] * idx[None, :]) % m).astype(np.float64) / m
        return np.cos(ang) / np.sqrt(m), np.sin(ang) / np.sqrt(m)

    cn, sn = cs(n)
    cc, sc = cs(FNET_GROUP)
    return (np.concatenate([cn, -sn], axis=1).astype(np.float32),
            cc.astype(np.float32), sc.astype(np.float32))


def _fourier_kernel(n, u_ref, w_ref, cc_ref, sc_ref, o_ref, x_ref):
    @pl.when(pl.program_id(1) == 0)
    def _():
        for g in range(N_FNET_GROUPS):
            cols = slice(g * FNET_GROUP, (g + 1) * FNET_GROUP)
            ug = u_ref[:, cols]
            x_ref[0:n, cols] = jnp.dot(ug, cc_ref[...], preferred_element_type=F32).astype(BF16)
            x_ref[n:2 * n, cols] = jnp.dot(ug, sc_ref[...], preferred_element_type=F32).astype(BF16)

    o_ref[...] = jnp.dot(w_ref[...], x_ref[...], preferred_element_type=F32).astype(BF16)


def _fourier(rest, batch, tr):
    t = rest.shape[0]
    n = t // batch
    w, cc, sc = (jnp.asarray(a).astype(BF16) for a in _dft_tables(n))
    nr = n // tr
    return pl.pallas_call(
        functools.partial(_fourier_kernel, n),
        grid=(batch, nr),
        in_specs=[
            pl.BlockSpec((n, FNET_W), lambda b, r: (b, REST_UB_BLOCK)),
            pl.BlockSpec((tr, 2 * n), lambda b, r: (r, 0)),
            pl.BlockSpec((FNET_GROUP, FNET_GROUP), lambda b, r: (0, 0)),
            pl.BlockSpec((FNET_GROUP, FNET_GROUP), lambda b, r: (0, 0)),
        ],
        out_specs=pl.BlockSpec((tr, FNET_W), lambda b, r: (b * nr + r, 0)),
        out_shape=jax.ShapeDtypeStruct((t, FNET_W), BF16),
        scratch_shapes=[pltpu.VMEM((2 * n, FNET_W), BF16)],
        compiler_params=pltpu.CompilerParams(
            dimension_semantics=("arbitrary", "arbitrary"), vmem_limit_bytes=VMEM_LIMIT),
        name="fourier",
    )(rest, w, cc, sc)


def _merge_kernel(final, attn_ref, za_ref, fm_ref, zb_ref, ga_ref, gb_ref, x_ref, gate_ref,
                  wpa_ref, wpb_ref, wo_ref, fg_ref, o_ref):
    ta = (attn_ref[...].astype(F32) * _silu(za_ref[...].astype(F32))).astype(BF16)
    ya = jnp.dot(ta, wpa_ref[...], preferred_element_type=F32)
    tb = (fm_ref[...].astype(F32) * _silu(zb_ref[...].astype(F32))).astype(BF16)
    yb = jnp.dot(tb, wpb_ref[...], preferred_element_type=F32)
    mix = (jax.nn.sigmoid(ga_ref[...].astype(F32)) * ya
           + jax.nn.sigmoid(gb_ref[...].astype(F32)) * yb).astype(BF16)
    y = jnp.dot(mix, wo_ref[...], preferred_element_type=F32)
    out = x_ref[...] + gate_ref[0] * y
    if final:
        ms = jnp.mean(out * out, axis=-1, keepdims=True)
        out = out * lax.rsqrt(ms + EPS) * fg_ref[...]
    o_ref[...] = out


def _merge(rest, attn, fm, x2d, mod, mod_row_of_tile, w_pa, w_pb, w_o, final_g, final, tm):
    t = x2d.shape[0]

    def rows(width, block=0):
        return pl.BlockSpec((tm, width), lambda i: (i, block))

    def whole(shape):
        return pl.BlockSpec(shape, lambda i: (0, 0), pipeline_mode=pl.Buffered(1))

    return pl.pallas_call(
        functools.partial(_merge_kernel, final),
        grid=(t // tm,),
        in_specs=[
            rows(ATT_W), rows(ATT_W, REST_ZA_BLOCK), rows(FNET_W), rows(FNET_W, REST_ZB_BLOCK),
            rows(D_MODEL, REST_GA_BLOCK), rows(D_MODEL, REST_GB_BLOCK),
            rows(D_MODEL),
            pl.BlockSpec((1, 1, D_MODEL), lambda i: (mod_row_of_tile(i) * 3 + 2, 0, 0)),
            whole((ATT_W, D_MODEL)), whole((FNET_W, D_MODEL)), whole((D_MODEL, D_MODEL)),
            pl.BlockSpec((1, D_MODEL), lambda i: (0, 0)),
        ],
        out_specs=rows(D_MODEL),
        out_shape=jax.ShapeDtypeStruct((t, D_MODEL), F32),
        compiler_params=pltpu.CompilerParams(
            dimension_semantics=("arbitrary",), vmem_limit_bytes=VMEM_LIMIT),
        name="merge",
    )(attn, rest, fm, rest, rest, rest, x2d, mod, w_pa, w_pb, w_o, final_g)


def _rope_tables(n_tokens):
    t = np.arange(n_tokens)
    pos = np.stack([t // GRID_W, t % GRID_W], axis=1).astype(np.float32)
    inv = (np.float32(ROPE_THETA) ** (-np.arange(ROPE_NFREQ, dtype=np.float32) / ROPE_NFREQ))
    ang = (pos[:, :, None] * inv[None, None, :]).astype(np.float32).astype(np.float64)
    cos, sin = np.cos(ang), np.sin(ang)
    zero = np.zeros_like(sin)
    c = np.stack([cos, cos], axis=2).reshape(n_tokens, HEAD_DIM)
    s1 = np.stack([-sin, zero], axis=2).reshape(n_tokens, HEAD_DIM)
    s2 = np.stack([zero, sin], axis=2).reshape(n_tokens, HEAD_DIM)
    return tuple(jnp.asarray(a.astype(np.float32)) for a in (c, s1, s2))


def _identity_tables(n_tokens):
    one = jnp.ones((n_tokens, HEAD_DIM), F32)
    zero = jnp.zeros((n_tokens, HEAD_DIM), F32)
    return one, zero, zero


def kernel(x, c, ctx, c_ctx, w_ada, b_ada, norm_g, w_in, q_norm_g, k_norm_g,
           w_proj_a, w_proj_b, w_out, final_g):
    batch, n_lat, d = x.shape
    n_ctx = ctx.shape[1]
    tm_lat, tm_ctx, tm_merge = 1024, 512, 256
    ctx_row = batch

    cvec = jnp.concatenate(
        [c, c_ctx[None], jnp.zeros((MOD_ROWS - batch - 1, d), F32)], axis=0)
    mods = _ada(cvec, w_ada, b_ada)

    rope = _rope_tables(n_lat)
    no_rope = _identity_tables(tm_ctx)
    lat_tiles = n_lat // tm_lat
    merge_tiles = n_lat // tm_merge

    xs = x.reshape(batch * n_lat, d)
    cs = ctx.reshape(batch * n_ctx, d)
    for l in range(DEPTH):
        last = l == DEPTH - 1
        mod = mods[l].reshape(MOD_ROWS * 3, 1, d)
        w_l = w_in[l].astype(BF16)
        w_pa, w_pb, w_o = (w[l].astype(BF16) for w in (w_proj_a, w_proj_b, w_out))
        g_l = norm_g[l].reshape(1, d)
        heads_per_tile = COL_TILE // HEAD_DIM
        qg = jnp.tile(q_norm_g[l] * Q_SCALE, heads_per_tile).reshape(1, COL_TILE)
        kg = jnp.tile(k_norm_g[l], heads_per_tile).reshape(1, COL_TILE)
        fg = final_g.reshape(1, d)

        lo, hi = (K_TILE, V_TILE + 1) if last else (0, N_COL_TILES)
        pc = _in_proj(cs, mod, lambda i: ctx_row, g_l, w_l, qg, kg, no_rope, lambda i: 0,
                      lo, hi, tm_ctx)
        p = _in_proj(xs, mod, lambda i: i // lat_tiles, g_l, w_l, qg, kg, rope,
                     lambda i: i % lat_tiles, 0, N_COL_TILES, tm_lat)

        attn = _attention(p["q"], [(p["k"], p["vt"]), (pc["k"], pc["vt"])], batch, 256)
        fm = _fourier(p["rest"], batch, 512)
        new_xs = _merge(p["rest"], attn, fm, xs, mod, lambda i: i // merge_tiles,
                        w_pa, w_pb, w_o, fg, last, tm_merge)
        if not last:
            attn_c = _attention(pc["q"], [(pc["k"], pc["vt"])], batch, n_ctx)
            fm_c = _fourier(pc["rest"], batch, n_ctx)
            cs = _merge(pc["rest"], attn_c, fm_c, cs, mod, lambda i: ctx_row,
                        w_pa, w_pb, w_o, fg, False, tm_merge)
        xs = new_xs
    return xs.reshape(batch, n_lat, d)
```

```python
import functools
import math

import numpy as np
import jax
import jax.numpy as jnp
from jax import lax
from jax.experimental import pallas as pl
from jax.experimental.pallas import tpu as pltpu

F32 = jnp.float32
BF16 = jnp.bfloat16

D_MODEL = 2048
DEPTH = 2
CTX_LEN = 256
GRID_W = 64
HEAD_DIM = 128
ATT_W = (3 * D_MODEL) // 4
N_Q_HEADS = ATT_W // HEAD_DIM
N_KV_HEADS = 4
Q_PER_KV = N_Q_HEADS // N_KV_HEADS
KV_W = N_KV_HEADS * HEAD_DIM
FNET_W = D_MODEL // 4
N_FNET_GROUPS = 4
FNET_GROUP = FNET_W // N_FNET_GROUPS
ROPE_THETA = 10000.0
ROPE_NFREQ = HEAD_DIM // 4
EPS = 1e-6
IN_W = 2 * ATT_W + 2 * KV_W + 2 * FNET_W + 2 * D_MODEL
Q_SCALE = HEAD_DIM ** -0.5 * math.log2(math.e)

V7X_VMEM_BYTES = 64 * 1024 * 1024
VMEM_LIMIT = V7X_VMEM_BYTES - 8 * 1024 * 1024

COL_TILE = 512
N_COL_TILES = IN_W // COL_TILE
K_TILE = ATT_W // COL_TILE
V_TILE = K_TILE + 1
ZB_TILE = (2 * ATT_W + 2 * KV_W + FNET_W) // COL_TILE
REST_W = IN_W - ATT_W - 2 * KV_W
REST_ZA_BLOCK = 0
REST_UB_BLOCK = ATT_W // FNET_W
REST_GA_BLOCK = 1
REST_GB_BLOCK = 2
REST_ZB_BLOCK = REST_W // FNET_W - 1
MOD_ROWS = 8


def _silu(x):
    return x * jax.nn.sigmoid(x)


def _ada_kernel(cv_ref, w_ref, b_ref, o_ref):
    s = _silu(cv_ref[...]).astype(BF16)
    o_ref[0] = jnp.dot(s, w_ref[0].astype(BF16), preferred_element_type=F32) + b_ref[0]


def _ada(cvec, w_ada, b_ada):
    tn = 1024
    n = 3 * D_MODEL
    return pl.pallas_call(
        _ada_kernel,
        grid=(DEPTH, n // tn),
        in_specs=[
            pl.BlockSpec((MOD_ROWS, D_MODEL), lambda l, j: (0, 0)),
            pl.BlockSpec((1, D_MODEL, tn), lambda l, j: (l, 0, j)),
            pl.BlockSpec((1, 1, tn), lambda l, j: (l, 0, j)),
        ],
        out_specs=pl.BlockSpec((1, MOD_ROWS, tn), lambda l, j: (l, 0, j)),
        out_shape=jax.ShapeDtypeStruct((DEPTH, MOD_ROWS, n), F32),
        compiler_params=pltpu.CompilerParams(
            dimension_semantics=("arbitrary", "arbitrary"), vmem_limit_bytes=VMEM_LIMIT),
        name="ada",
    )(cvec, w_ada, b_ada.reshape(DEPTH, 1, n))


def _weight_tile(step):
    return jnp.where(step < ZB_TILE, step, jnp.where(step < N_COL_TILES - 1, step + 1, ZB_TILE))


def _split_bf16(x):
    hi = x.astype(BF16)
    return hi, (x - hi.astype(F32)).astype(BF16)


def _head_norm_rope(acc, head_mean, g, c, s1, s2):
    sq_hi, sq_lo = _split_bf16(acc * acc)
    ms = (jnp.dot(sq_hi, head_mean, preferred_element_type=F32)
          + jnp.dot(sq_lo, head_mean, preferred_element_type=F32))
    y = acc * lax.rsqrt(ms + EPS) * g
    outs = []
    for hh in range(COL_TILE // HEAD_DIM):
        yh = y[:, hh * HEAD_DIM:(hh + 1) * HEAD_DIM]
        outs.append(yh * c + pltpu.roll(yh, HEAD_DIM - ROPE_NFREQ, 1) * s1
                    + pltpu.roll(yh, ROPE_NFREQ, 1) * s2)
    return jnp.concatenate(outs, axis=-1)


def _in_proj_kernel(step_lo, names, x_ref, shift_ref, scale_ref, g_ref, w_ref, qg_ref, kg_ref,
                    hm_ref, c_ref, s1_ref, s2_ref, *rest):
    out = dict(zip(names, rest[:-1]))
    h_ref = rest[-1]
    step = pl.program_id(1) + step_lo

    @pl.when(pl.program_id(1) == 0)
    def _():
        xf = x_ref[...]
        ms = jnp.mean(xf * xf, axis=-1, keepdims=True)
        y = xf * lax.rsqrt(ms + EPS) * g_ref[...]
        h_ref[...] = (y * (1.0 + scale_ref[0]) + shift_ref[0]).astype(BF16)

    def acc():
        return jnp.dot(h_ref[...], w_ref[...], preferred_element_type=F32)

    if "q" in out:
        @pl.when(step < K_TILE)
        def _():
            out["q"][...] = _head_norm_rope(
                acc(), hm_ref[...], qg_ref[...], c_ref[...], s1_ref[...], s2_ref[...]).astype(BF16)

    if "k" in out:
        @pl.when(step == K_TILE)
        def _():
            out["k"][...] = _head_norm_rope(
                acc(), hm_ref[...], kg_ref[...], c_ref[...], s1_ref[...], s2_ref[...]).astype(BF16)

    if "vt" in out:
        @pl.when(step == V_TILE)
        def _():
            out["vt"][...] = acc().T.astype(BF16)

    if "rest" in out:
        @pl.when(step > V_TILE)
        def _():
            out["rest"][...] = acc().astype(BF16)


def _in_proj(x2d, mod, mod_row_of_tile, norm_g, w, q_g, k_g, tables, table_tile_of_tile,
             step_lo, step_hi, tm):
    t = x2d.shape[0]
    c_tab, s1_tab, s2_tab = tables
    n_rest = REST_W // COL_TILE
    heads_per_tile = COL_TILE // HEAD_DIM
    head_mean = jnp.asarray(
        np.kron(np.eye(heads_per_tile), np.full((HEAD_DIM, HEAD_DIM), 1.0 / HEAD_DIM)), BF16)
    specs = {
        "q": (pl.BlockSpec((tm, COL_TILE), lambda i, j: (i, jnp.clip(j + step_lo, 0, K_TILE - 1))),
              jax.ShapeDtypeStruct((t, ATT_W), BF16), step_lo < K_TILE),
        "k": (pl.BlockSpec((tm, COL_TILE), lambda i, j: (i, 0)),
              jax.ShapeDtypeStruct((t, KV_W), BF16), step_lo <= K_TILE < step_hi),
        "vt": (pl.BlockSpec((COL_TILE, tm), lambda i, j: (0, i)),
               jax.ShapeDtypeStruct((KV_W, t), BF16), step_lo <= V_TILE < step_hi),
        "rest": (pl.BlockSpec((tm, COL_TILE),
                              lambda i, j: (i, jnp.clip(j + step_lo - V_TILE - 1, 0, n_rest - 1))),
                 jax.ShapeDtypeStruct((t, REST_W), BF16), step_hi > V_TILE + 1),
    }
    names = tuple(n for n, (_, _, present) in specs.items() if present)
    tab_spec = pl.BlockSpec((tm, HEAD_DIM), lambda i, j: (table_tile_of_tile(i), 0))
    outs = pl.pallas_call(
        functools.partial(_in_proj_kernel, step_lo, names),
        grid=(t // tm, step_hi - step_lo),
        in_specs=[
            pl.BlockSpec((tm, D_MODEL), lambda i, j: (i, 0)),
            pl.BlockSpec((1, 1, D_MODEL), lambda i, j: (mod_row_of_tile(i) * 3, 0, 0)),
            pl.BlockSpec((1, 1, D_MODEL), lambda i, j: (mod_row_of_tile(i) * 3 + 1, 0, 0)),
            pl.BlockSpec((1, D_MODEL), lambda i, j: (0, 0)),
            pl.BlockSpec((D_MODEL, COL_TILE), lambda i, j: (0, _weight_tile(j + step_lo))),
            pl.BlockSpec((1, COL_TILE), lambda i, j: (0, 0)),
            pl.BlockSpec((1, COL_TILE), lambda i, j: (0, 0)),
            pl.BlockSpec((COL_TILE, COL_TILE), lambda i, j: (0, 0)),
            tab_spec, tab_spec, tab_spec,
        ],
        out_specs=[specs[n][0] for n in names],
        out_shape=[specs[n][1] for n in names],
        scratch_shapes=[pltpu.VMEM((tm, D_MODEL), BF16)],
        compiler_params=pltpu.CompilerParams(
            dimension_semantics=("arbitrary", "arbitrary"), vmem_limit_bytes=VMEM_LIMIT),
        name="in_proj",
    )(x2d, mod, mod, norm_g, w, q_g, k_g, head_mean, c_tab, s1_tab, s2_tab)
    return dict(zip(names, outs))


def _col_reduce8(x, op):
    rows, w = x.shape
    return op(x.reshape(rows // 8, 8, w), axis=0)


def _attn_kernel(chunks, tq, q_ref, *rest):
    n_sets = 1 + max(c[0] for c in chunks)
    k_refs, vt_refs = rest[:n_sets], rest[n_sets:2 * n_sets]
    o_ref, s_scr, m_scr = rest[2 * n_sets:]
    width = Q_PER_KV * tq

    @pl.when(pl.program_id(0) == 0)
    def _():
        s_scr[...] = jnp.zeros_like(s_scr)
        m_scr[...] = jnp.zeros_like(m_scr)

    q3 = q_ref[...]
    qs = jnp.concatenate(
        [q3[:, i * HEAD_DIM:(i + 1) * HEAD_DIM] for i in range(Q_PER_KV)], axis=0)
    q_t = qs.astype(F32).T.astype(BF16)
    m_prev = m_scr[...]
    m_run = jnp.full((8, width), -jnp.inf, F32)
    l8 = jnp.zeros((8, width), F32)
    acc = jnp.zeros((HEAD_DIM, width), F32)
    for t, r0, rows, s0 in chunks:
        p = jnp.exp2(s_scr[s0:s0 + rows, :] - m_prev)
        l8 = l8 + _col_reduce8(p, jnp.sum)
        acc = acc + jnp.dot(vt_refs[t][:, r0:r0 + rows], p.astype(BF16),
                            preferred_element_type=F32)
        s_new = jnp.dot(k_refs[t][r0:r0 + rows, :], q_t, preferred_element_type=F32)
        s_scr[s0:s0 + rows, :] = s_new
        m_run = jnp.maximum(m_run, _col_reduce8(s_new, jnp.max))
    m_scr[...] = m_run.max(axis=0, keepdims=True)
    o_t = acc / l8.sum(axis=0, keepdims=True)
    for i in range(Q_PER_KV):
        o_ref[:, i * HEAD_DIM:(i + 1) * HEAD_DIM] = o_t[:, i * tq:(i + 1) * tq].T.astype(BF16)


def _attention(q, kv_sets, batch, tq, key_chunk):
    t = q.shape[0]
    nq = t // batch // tq
    n_tiles = batch * N_KV_HEADS * nq
    gw = Q_PER_KV * HEAD_DIM

    def tile(n):
        return n // (N_KV_HEADS * nq), (n // nq) % N_KV_HEADS, n % nq

    def score_tile(n):
        return tile(jnp.minimum(n, n_tiles - 1))

    def out_tile(n):
        return tile(jnp.maximum(n - 1, 0))

    def q_map(n):
        b, g, i = score_tile(n)
        return b * nq + i, g

    def o_map(n):
        b, g, i = out_tile(n)
        return b * nq + i, g

    def k_map(n):
        b, g, _ = score_tile(n)
        return b, g

    def vt_map(n):
        b, g, _ = out_tile(n)
        return g, b

    chunks, k_specs, vt_specs, s_rows = [], [], [], 0
    for s, (k, _) in enumerate(kv_sets):
        m_t = k.shape[0] // batch
        step = min(key_chunk, m_t)
        for r0 in range(0, m_t, step):
            chunks.append((s, r0, step, s_rows + r0))
        s_rows += m_t
        k_specs.append(pl.BlockSpec((m_t, HEAD_DIM), k_map))
        vt_specs.append(pl.BlockSpec((HEAD_DIM, m_t), vt_map))
    return pl.pallas_call(
        functools.partial(_attn_kernel, tuple(chunks), tq),
        grid=(n_tiles + 1,),
        in_specs=[pl.BlockSpec((tq, gw), q_map)] + k_specs + vt_specs,
        out_specs=pl.BlockSpec((tq, gw), o_map),
        out_shape=jax.ShapeDtypeStruct((t, ATT_W), BF16),
        scratch_shapes=[pltpu.VMEM((s_rows, Q_PER_KV * tq), F32),
                        pltpu.VMEM((1, Q_PER_KV * tq), F32)],
        compiler_params=pltpu.CompilerParams(
            dimension_semantics=("arbitrary",), vmem_limit_bytes=VMEM_LIMIT),
        name="attn",
    )(q, *[k for k, _ in kv_sets], *[vt for _, vt in kv_sets])


def _dft_tables(n):
    def cs(m):
        idx = np.arange(m, dtype=np.int64)
        ang = 2.0 * np.pi * ((idx[:, None] * idx[None, :]) % m).astype(np.float64) / m
        return np.cos(ang) / np.sqrt(m), np.sin(ang) / np.sqrt(m)

    cn, sn = cs(n)
    cc, sc = cs(FNET_GROUP)
    return (np.concatenate([cn, -sn], axis=1).astype(np.float32),
            cc.astype(np.float32), sc.astype(np.float32))


def _fourier_kernel(n, u_ref, w_ref, cc_ref, sc_ref, o_ref, x_ref):
    @pl.when(pl.program_id(1) == 0)
    def _():
        for g in range(N_FNET_GROUPS):
            cols = slice(g * FNET_GROUP, (g + 1) * FNET_GROUP)
            ug = u_ref[:, cols]
            x_ref[0:n, cols] = jnp.dot(ug, cc_ref[...], preferred_element_type=F32).astype(BF16)
            x_ref[n:2 * n, cols] = jnp.dot(ug, sc_ref[...], preferred_element_type=F32).astype(BF16)

    o_ref[...] = jnp.dot(w_ref[...], x_ref[...], preferred_element_type=F32).astype(BF16)


def _fourier(rest, batch, tr):
    t = rest.shape[0]
    n = t // batch
    w, cc, sc = (jnp.asarray(a).astype(BF16) for a in _dft_tables(n))
    nr = n // tr
    return pl.pallas_call(
        functools.partial(_fourier_kernel, n),
        grid=(batch, nr),
        in_specs=[
            pl.BlockSpec((n, FNET_W), lambda b, r: (b, REST_UB_BLOCK)),
            pl.BlockSpec((tr, 2 * n), lambda b, r: (r, 0)),
            pl.BlockSpec((FNET_GROUP, FNET_GROUP), lambda b, r: (0, 0)),
            pl.BlockSpec((FNET_GROUP, FNET_GROUP), lambda b, r: (0, 0)),
        ],
        out_specs=pl.BlockSpec((tr, FNET_W), lambda b, r: (b * nr + r, 0)),
        out_shape=jax.ShapeDtypeStruct((t, FNET_W), BF16),
        scratch_shapes=[pltpu.VMEM((2 * n, FNET_W), BF16)],
        compiler_params=pltpu.CompilerParams(
            dimension_semantics=("arbitrary", "arbitrary"), vmem_limit_bytes=VMEM_LIMIT),
        name="fourier",
    )(rest, w, cc, sc)


def _merge_kernel(final, attn_ref, za_ref, fm_ref, zb_ref, ga_ref, gb_ref, x_ref, gate_ref,
                  wpa_ref, wpb_ref, wo_ref, fg_ref, o_ref):
    ta = (attn_ref[...].astype(F32) * _silu(za_ref[...].astype(F32))).astype(BF16)
    ya = jnp.dot(ta, wpa_ref[...], preferred_element_type=F32)
    tb = (fm_ref[...].astype(F32) * _silu(zb_ref[...].astype(F32))).astype(BF16)
    yb = jnp.dot(tb, wpb_ref[...], preferred_element_type=F32)
    mix = (jax.nn.sigmoid(ga_ref[...].astype(F32)) * ya
           + jax.nn.sigmoid(gb_ref[...].astype(F32)) * yb).astype(BF16)
    y = jnp.dot(mix, wo_ref[...], preferred_element_type=F32)
    out = x_ref[...] + gate_ref[0] * y
    if final:
        ms = jnp.mean(out * out, axis=-1, keepdims=True)
        out = out * lax.rsqrt(ms + EPS) * fg_ref[...]
    o_ref[...] = out


def _merge(rest, attn, fm, x2d, mod, mod_row_of_tile, w_pa, w_pb, w_o, final_g, final, tm):
    t = x2d.shape[0]

    def rows(width, block=0):
        return pl.BlockSpec((tm, width), lambda i: (i, block))

    def whole(shape):
        return pl.BlockSpec(shape, lambda i: (0, 0), pipeline_mode=pl.Buffered(1))

    return pl.pallas_call(
        functools.partial(_merge_kernel, final),
        grid=(t // tm,),
        in_specs=[
            rows(ATT_W), rows(ATT_W, REST_ZA_BLOCK), rows(FNET_W), rows(FNET_W, REST_ZB_BLOCK),
            rows(D_MODEL, REST_GA_BLOCK), rows(D_MODEL, REST_GB_BLOCK),
            rows(D_MODEL),
            pl.BlockSpec((1, 1, D_MODEL), lambda i: (mod_row_of_tile(i) * 3 + 2, 0, 0)),
            whole((ATT_W, D_MODEL)), whole((FNET_W, D_MODEL)), whole((D_MODEL, D_MODEL)),
            pl.BlockSpec((1, D_MODEL), lambda i: (0, 0)),
        ],
        out_specs=rows(D_MODEL),
        out_shape=jax.ShapeDtypeStruct((t, D_MODEL), F32),
        compiler_params=pltpu.CompilerParams(
            dimension_semantics=("arbitrary",), vmem_limit_bytes=VMEM_LIMIT),
        name="merge",
    )(attn, rest, fm, rest, rest, rest, x2d, mod, w_pa, w_pb, w_o, final_g)


def _rope_tables(n_tokens):
    t = np.arange(n_tokens)
    pos = np.stack([t // GRID_W, t % GRID_W], axis=1).astype(np.float32)
    inv = (np.float32(ROPE_THETA) ** (-np.arange(ROPE_NFREQ, dtype=np.float32) / ROPE_NFREQ))
    ang = (pos[:, :, None] * inv[None, None, :]).astype(np.float32).astype(np.float64)
    cos, sin = np.cos(ang), np.sin(ang)
    zero = np.zeros_like(sin)
    c = np.stack([cos, cos], axis=2).reshape(n_tokens, HEAD_DIM)
    s1 = np.stack([-sin, zero], axis=2).reshape(n_tokens, HEAD_DIM)
    s2 = np.stack([zero, sin], axis=2).reshape(n_tokens, HEAD_DIM)
    return tuple(jnp.asarray(a.astype(np.float32)) for a in (c, s1, s2))


def _identity_tables(n_tokens):
    one = jnp.ones((n_tokens, HEAD_DIM), F32)
    zero = jnp.zeros((n_tokens, HEAD_DIM), F32)
    return one, zero, zero


def kernel(x, c, ctx, c_ctx, w_ada, b_ada, norm_g, w_in, q_norm_g, k_norm_g,
           w_proj_a, w_proj_b, w_out, final_g):
    batch, n_lat, d = x.shape
    n_ctx = ctx.shape[1]
    tm_lat, tm_ctx, tm_merge = 1024, 512, 256
    ctx_row = batch

    cvec = jnp.concatenate(
        [c, c_ctx[None], jnp.zeros((MOD_ROWS - batch - 1, d), F32)], axis=0)
    mods = _ada(cvec, w_ada, b_ada)

    rope = _rope_tables(n_lat)
    no_rope = _identity_tables(tm_ctx)
    lat_tiles = n_lat // tm_lat
    merge_tiles = n_lat // tm_merge

    xs = x.reshape(batch * n_lat, d)
    cs = ctx.reshape(batch * n_ctx, d)
    for l in range(DEPTH):
        last = l == DEPTH - 1
        mod = mods[l].reshape(MOD_ROWS * 3, 1, d)
        w_l = w_in[l].astype(BF16)
        w_pa, w_pb, w_o = (w[l].astype(BF16) for w in (w_proj_a, w_proj_b, w_out))
        g_l = norm_g[l].reshape(1, d)
        heads_per_tile = COL_TILE // HEAD_DIM
        qg = jnp.tile(q_norm_g[l] * Q_SCALE, heads_per_tile).reshape(1, COL_TILE)
        kg = jnp.tile(k_norm_g[l], heads_per_tile).reshape(1, COL_TILE)
        fg = final_g.reshape(1, d)

        lo, hi = (K_TILE, V_TILE + 1) if last else (0, N_COL_TILES)
        pc = _in_proj(cs, mod, lambda i: ctx_row, g_l, w_l, qg, kg, no_rope, lambda i: 0,
                      lo, hi, tm_ctx)
        p = _in_proj(xs, mod, lambda i: i // lat_tiles, g_l, w_l, qg, kg, rope,
                     lambda i: i % lat_tiles, 0, N_COL_TILES, tm_lat)

        attn = _attention(p["q"], [(p["k"], p["vt"]), (pc["k"], pc["vt"])], batch, 256, 512)
        fm = _fourier(p["rest"], batch, 512)
        new_xs = _merge(p["rest"], attn, fm, xs, mod, lambda i: i // merge_tiles,
                        w_pa, w_pb, w_o, fg, last, tm_merge)
        if not last:
            attn_c = _attention(pc["q"], [(pc["k"], pc["vt"])], batch, n_ctx, n_ctx)
            fm_c = _fourier(pc["rest"], batch, n_ctx)
            cs = _merge(pc["rest"], attn_c, fm_c, cs, mod, lambda i: ctx_row,
                        w_pa, w_pb, w_o, fg, False, tm_merge)
        xs = new_xs
    return xs.reshape(batch, n_lat, d)
```

```python
import functools
import math

import numpy as np
import jax
import jax.numpy as jnp
from jax import lax
from jax.experimental import pallas as pl
from jax.experimental.pallas import tpu as pltpu

F32 = jnp.float32
BF16 = jnp.bfloat16

D_MODEL = 2048
DEPTH = 2
CTX_LEN = 256
GRID_W = 64
HEAD_DIM = 128
ATT_W = (3 * D_MODEL) // 4
N_Q_HEADS = ATT_W // HEAD_DIM
N_KV_HEADS = 4
Q_PER_KV = N_Q_HEADS // N_KV_HEADS
KV_W = N_KV_HEADS * HEAD_DIM
FNET_W = D_MODEL // 4
N_FNET_GROUPS = 4
FNET_GROUP = FNET_W // N_FNET_GROUPS
ROPE_THETA = 10000.0
ROPE_NFREQ = HEAD_DIM // 4
EPS = 1e-6
IN_W = 2 * ATT_W + 2 * KV_W + 2 * FNET_W + 2 * D_MODEL
Q_SCALE = HEAD_DIM ** -0.5 * math.log2(math.e)

V7X_VMEM_BYTES = 64 * 1024 * 1024
VMEM_LIMIT = V7X_VMEM_BYTES - 8 * 1024 * 1024

COL_TILE = 512
N_COL_TILES = IN_W // COL_TILE
K_TILE = ATT_W // COL_TILE
V_TILE = K_TILE + 1
ZB_TILE = (2 * ATT_W + 2 * KV_W + FNET_W) // COL_TILE
REST_W = IN_W - ATT_W - 2 * KV_W
REST_ZA_BLOCK = 0
REST_UB_BLOCK = ATT_W // FNET_W
REST_GA_BLOCK = 1
REST_GB_BLOCK = 2
REST_ZB_BLOCK = REST_W // FNET_W - 1
MOD_ROWS = 8


def _silu(x):
    return x * jax.nn.sigmoid(x)


def _ada_kernel(cv_ref, w_ref, b_ref, o_ref):
    s = _silu(cv_ref[...]).astype(BF16)
    o_ref[0] = jnp.dot(s, w_ref[0].astype(BF16), preferred_element_type=F32) + b_ref[0]


def _ada(cvec, w_ada, b_ada):
    tn = 1024
    n = 3 * D_MODEL
    return pl.pallas_call(
        _ada_kernel,
        grid=(DEPTH, n // tn),
        in_specs=[
            pl.BlockSpec((MOD_ROWS, D_MODEL), lambda l, j: (0, 0)),
            pl.BlockSpec((1, D_MODEL, tn), lambda l, j: (l, 0, j)),
            pl.BlockSpec((1, 1, tn), lambda l, j: (l, 0, j)),
        ],
        out_specs=pl.BlockSpec((1, MOD_ROWS, tn), lambda l, j: (l, 0, j)),
        out_shape=jax.ShapeDtypeStruct((DEPTH, MOD_ROWS, n), F32),
        compiler_params=pltpu.CompilerParams(
            dimension_semantics=("arbitrary", "arbitrary"), vmem_limit_bytes=VMEM_LIMIT),
        name="ada",
    )(cvec, w_ada, b_ada.reshape(DEPTH, 1, n))


def _weight_tile(step):
    return jnp.where(step < ZB_TILE, step, jnp.where(step < N_COL_TILES - 1, step + 1, ZB_TILE))


def _split_bf16(x):
    hi = x.astype(BF16)
    return hi, (x - hi.astype(F32)).astype(BF16)


def _head_norm_rope(acc, head_mean, g, c, s1, s2):
    sq_hi, sq_lo = _split_bf16(acc * acc)
    ms = (jnp.dot(sq_hi, head_mean, preferred_element_type=F32)
          + jnp.dot(sq_lo, head_mean, preferred_element_type=F32))
    y = acc * lax.rsqrt(ms + EPS) * g
    outs = []
    for hh in range(COL_TILE // HEAD_DIM):
        yh = y[:, hh * HEAD_DIM:(hh + 1) * HEAD_DIM]
        outs.append(yh * c + pltpu.roll(yh, HEAD_DIM - ROPE_NFREQ, 1) * s1
                    + pltpu.roll(yh, ROPE_NFREQ, 1) * s2)
    return jnp.concatenate(outs, axis=-1)


def _modnorm(xf, g, scale, shift):
    ms = jnp.mean(xf * xf, axis=-1, keepdims=True)
    return xf * lax.rsqrt(ms + EPS) * g * (1.0 + scale) + shift


def _modnorm_kernel(x_ref, shift_ref, scale_ref, g_ref, h_ref):
    h_ref[...] = _modnorm(x_ref[...], g_ref[0], scale_ref[0], shift_ref[0]).astype(BF16)


def _mod_specs(layer, mod_row_of_tile, parts):
    return [pl.BlockSpec((1, 1, D_MODEL),
                         lambda i, p=p: ((layer * MOD_ROWS + mod_row_of_tile(i)) * 3 + p, 0, 0))
            for p in parts]


MOD_SHIFT, MOD_SCALE, MOD_GATE = 0, 1, 2


def _modnorm_call(x2d, mods, layer, mod_row_of_tile, norm_g, tm):
    t = x2d.shape[0]
    return pl.pallas_call(
        _modnorm_kernel,
        grid=(t // tm,),
        in_specs=[pl.BlockSpec((tm, D_MODEL), lambda i: (i, 0))]
        + _mod_specs(layer, mod_row_of_tile, (MOD_SHIFT, MOD_SCALE))
        + [pl.BlockSpec((1, 1, D_MODEL), lambda i: (layer, 0, 0))],
        out_specs=pl.BlockSpec((tm, D_MODEL), lambda i: (i, 0)),
        out_shape=jax.ShapeDtypeStruct((t, D_MODEL), BF16),
        compiler_params=pltpu.CompilerParams(
            dimension_semantics=("arbitrary",), vmem_limit_bytes=VMEM_LIMIT),
        name="modnorm",
    )(x2d, mods, mods, norm_g)


def _in_proj_kernel(step_lo, names, sub, h_ref, w_ref, qg_ref, kg_ref, hm_ref,
                    c_ref, s1_ref, s2_ref, *out_refs):
    out = dict(zip(names, out_refs))
    step = pl.program_id(1) + step_lo
    row_tiles = [slice(r, r + sub) for r in range(0, h_ref.shape[0], sub)]

    def tiles():
        w = w_ref[0].astype(BF16)
        for rows in row_tiles:
            yield rows, jnp.dot(h_ref[rows, :], w, preferred_element_type=F32)

    def head_tiles(o_ref, g_ref):
        for rows, acc in tiles():
            o_ref[rows, :] = _head_norm_rope(
                acc, hm_ref[...], g_ref[0], c_ref[rows, :], s1_ref[rows, :],
                s2_ref[rows, :]).astype(BF16)

    if "q" in out:
        @pl.when(step < K_TILE)
        def _():
            head_tiles(out["q"], qg_ref)

    if "k" in out:
        @pl.when(step == K_TILE)
        def _():
            head_tiles(out["k"], kg_ref)

    if "vt" in out:
        @pl.when(step == V_TILE)
        def _():
            for rows, acc in tiles():
                out["vt"][:, rows] = acc.T.astype(BF16)

    if "rest" in out:
        @pl.when(step > V_TILE)
        def _():
            for rows, acc in tiles():
                out["rest"][rows, :] = acc.astype(BF16)


def _in_proj(h, w_in, layer, q_g, k_g, tables, step_lo, step_hi, group, sub):
    t = h.shape[0]
    c_tab, s1_tab, s2_tab = tables
    n_rest = REST_W // COL_TILE
    heads_per_tile = COL_TILE // HEAD_DIM
    head_mean = jnp.asarray(
        np.kron(np.eye(heads_per_tile), np.full((HEAD_DIM, HEAD_DIM), 1.0 / HEAD_DIM)), BF16)
    specs = {
        "q": (pl.BlockSpec((group, COL_TILE),
                           lambda i, j: (i, jnp.clip(j + step_lo, 0, K_TILE - 1))),
              jax.ShapeDtypeStruct((t, ATT_W), BF16), step_lo < K_TILE),
        "k": (pl.BlockSpec((group, COL_TILE), lambda i, j: (i, 0)),
              jax.ShapeDtypeStruct((t, KV_W), BF16), step_lo <= K_TILE < step_hi),
        "vt": (pl.BlockSpec((COL_TILE, group), lambda i, j: (0, i)),
               jax.ShapeDtypeStruct((KV_W, t), BF16), step_lo <= V_TILE < step_hi),
        "rest": (pl.BlockSpec((group, COL_TILE),
                              lambda i, j: (i, jnp.clip(j + step_lo - V_TILE - 1, 0, n_rest - 1))),
                 jax.ShapeDtypeStruct((t, REST_W), BF16), step_hi > V_TILE + 1),
    }
    names = tuple(n for n, (_, _, present) in specs.items() if present)
    tab_spec = pl.BlockSpec((group, HEAD_DIM), lambda i, j: (0, 0))
    gain_spec = pl.BlockSpec((1, 1, COL_TILE), lambda i, j: (layer, 0, 0))
    outs = pl.pallas_call(
        functools.partial(_in_proj_kernel, step_lo, names, sub),
        grid=(t // group, step_hi - step_lo),
        in_specs=[
            pl.BlockSpec((group, D_MODEL), lambda i, j: (i, 0), pipeline_mode=pl.Buffered(1)),
            pl.BlockSpec((1, D_MODEL, COL_TILE),
                         lambda i, j: (layer, 0, _weight_tile(j + step_lo))),
            gain_spec, gain_spec,
            pl.BlockSpec((COL_TILE, COL_TILE), lambda i, j: (0, 0)),
            tab_spec, tab_spec, tab_spec,
        ],
        out_specs=[specs[n][0] for n in names],
        out_shape=[specs[n][1] for n in names],
        compiler_params=pltpu.CompilerParams(
            dimension_semantics=("arbitrary", "arbitrary"), vmem_limit_bytes=VMEM_LIMIT),
        name="in_proj",
    )(h, w_in, q_g, k_g, head_mean, c_tab, s1_tab, s2_tab)
    return dict(zip(names, outs))


def _col_reduce8(x, op):
    rows, w = x.shape
    return op(x.reshape(rows // 8, 8, w), axis=0)


def _attn_kernel(chunks, tq, q_ref, *rest):
    n_sets = 1 + max(c[0] for c in chunks)
    k_refs, vt_refs = rest[:n_sets], rest[n_sets:2 * n_sets]
    o_ref, s_scr, m_scr = rest[2 * n_sets:]
    width = Q_PER_KV * tq

    @pl.when(pl.program_id(0) == 0)
    def _():
        s_scr[...] = jnp.zeros_like(s_scr)
        m_scr[...] = jnp.zeros_like(m_scr)

    q3 = q_ref[...]
    qs = jnp.concatenate(
        [q3[:, i * HEAD_DIM:(i + 1) * HEAD_DIM] for i in range(Q_PER_KV)], axis=0)
    q_t = qs.astype(F32).T.astype(BF16)
    m_prev = m_scr[...]
    m_run = jnp.full((8, width), -jnp.inf, F32)
    l8 = jnp.zeros((8, width), F32)
    acc = jnp.zeros((HEAD_DIM, width), F32)
    for t, r0, rows, s0 in chunks:
        p = jnp.exp2(s_scr[s0:s0 + rows, :] - m_prev)
        l8 = l8 + _col_reduce8(p, jnp.sum)
        acc = acc + jnp.dot(vt_refs[t][:, r0:r0 + rows], p.astype(BF16),
                            preferred_element_type=F32)
        s_new = jnp.dot(k_refs[t][r0:r0 + rows, :], q_t, preferred_element_type=F32)
        s_scr[s0:s0 + rows, :] = s_new
        m_run = jnp.maximum(m_run, _col_reduce8(s_new, jnp.max))
    m_scr[...] = m_run.max(axis=0, keepdims=True)
    o_t = acc / l8.sum(axis=0, keepdims=True)
    for i in range(Q_PER_KV):
        o_ref[:, i * HEAD_DIM:(i + 1) * HEAD_DIM] = o_t[:, i * tq:(i + 1) * tq].T.astype(BF16)


def _attention(q, kv_sets, batch, tq, key_chunk):
    t = q.shape[0]
    nq = t // batch // tq
    n_tiles = batch * N_KV_HEADS * nq
    gw = Q_PER_KV * HEAD_DIM

    def tile(n):
        return n // (N_KV_HEADS * nq), (n // nq) % N_KV_HEADS, n % nq

    def score_tile(n):
        return tile(jnp.minimum(n, n_tiles - 1))

    def out_tile(n):
        return tile(jnp.maximum(n - 1, 0))

    def q_map(n):
        b, g, i = score_tile(n)
        return b * nq + i, g

    def o_map(n):
        b, g, i = out_tile(n)
        return b * nq + i, g

    def k_map(n):
        b, g, _ = score_tile(n)
        return b, g

    def vt_map(n):
        b, g, _ = out_tile(n)
        return g, b

    chunks, k_specs, vt_specs, s_rows = [], [], [], 0
    for s, (k, _) in enumerate(kv_sets):
        m_t = k.shape[0] // batch
        step = min(key_chunk, m_t)
        for r0 in range(0, m_t, step):
            chunks.append((s, r0, step, s_rows + r0))
        s_rows += m_t
        k_specs.append(pl.BlockSpec((m_t, HEAD_DIM), k_map))
        vt_specs.append(pl.BlockSpec((HEAD_DIM, m_t), vt_map))
    return pl.pallas_call(
        functools.partial(_attn_kernel, tuple(chunks), tq),
        grid=(n_tiles + 1,),
        in_specs=[pl.BlockSpec((tq, gw), q_map)] + k_specs + vt_specs,
        out_specs=pl.BlockSpec((tq, gw), o_map),
        out_shape=jax.ShapeDtypeStruct((t, ATT_W), BF16),
        scratch_shapes=[pltpu.VMEM((s_rows, Q_PER_KV * tq), F32),
                        pltpu.VMEM((1, Q_PER_KV * tq), F32)],
        compiler_params=pltpu.CompilerParams(
            dimension_semantics=("arbitrary",), vmem_limit_bytes=VMEM_LIMIT),
        name="attn",
    )(q, *[k for k, _ in kv_sets], *[vt for _, vt in kv_sets])


def _dft_tables(n):
    def cs(m):
        idx = np.arange(m, dtype=np.int64)
        ang = 2.0 * np.pi * ((idx[:, None] * idx[None, :]) % m).astype(np.float64) / m
        return np.cos(ang) / np.sqrt(m), np.sin(ang) / np.sqrt(m)

    cn, sn = cs(n)
    cc, sc = cs(FNET_GROUP)
    return (np.concatenate([cn, -sn], axis=1).astype(np.float32),
            cc.astype(np.float32), sc.astype(np.float32))


def _fourier_kernel(n, u_ref, w_ref, cc_ref, sc_ref, o_ref, x_ref):
    @pl.when(pl.program_id(1) == 0)
    def _():
        for g in range(N_FNET_GROUPS):
            cols = slice(g * FNET_GROUP, (g + 1) * FNET_GROUP)
            ug = u_ref[:, cols]
            x_ref[0:n, cols] = jnp.dot(ug, cc_ref[...], preferred_element_type=F32).astype(BF16)
            x_ref[n:2 * n, cols] = jnp.dot(ug, sc_ref[...], preferred_element_type=F32).astype(BF16)

    o_ref[...] = jnp.dot(w_ref[...], x_ref[...], preferred_element_type=F32).astype(BF16)


def _fourier(rest, batch, tr):
    t = rest.shape[0]
    n = t // batch
    w, cc, sc = (jnp.asarray(a).astype(BF16) for a in _dft_tables(n))
    nr = n // tr
    return pl.pallas_call(
        functools.partial(_fourier_kernel, n),
        grid=(batch, nr),
        in_specs=[
            pl.BlockSpec((n, FNET_W), lambda b, r: (b, REST_UB_BLOCK)),
            pl.BlockSpec((tr, 2 * n), lambda b, r: (r, 0)),
            pl.BlockSpec((FNET_GROUP, FNET_GROUP), lambda b, r: (0, 0)),
            pl.BlockSpec((FNET_GROUP, FNET_GROUP), lambda b, r: (0, 0)),
        ],
        out_specs=pl.BlockSpec((tr, FNET_W), lambda b, r: (b * nr + r, 0)),
        out_shape=jax.ShapeDtypeStruct((t, FNET_W), BF16),
        scratch_shapes=[pltpu.VMEM((2 * n, FNET_W), BF16)],
        compiler_params=pltpu.CompilerParams(
            dimension_semantics=("arbitrary", "arbitrary"), vmem_limit_bytes=VMEM_LIMIT),
        name="fourier",
    )(rest, w, cc, sc)


def _merge_kernel(final, attn_ref, za_ref, fm_ref, zb_ref, ga_ref, gb_ref, x_ref, gate_ref,
                  wpa_ref, wpb_ref, wo_ref, *rest):
    ta = (attn_ref[...].astype(F32) * _silu(za_ref[...].astype(F32))).astype(BF16)
    ya = jnp.dot(ta, wpa_ref[0], preferred_element_type=F32)
    tb = (fm_ref[...].astype(F32) * _silu(zb_ref[...].astype(F32))).astype(BF16)
    yb = jnp.dot(tb, wpb_ref[0], preferred_element_type=F32)
    mix = (jax.nn.sigmoid(ga_ref[...].astype(F32)) * ya
           + jax.nn.sigmoid(gb_ref[...].astype(F32)) * yb).astype(BF16)
    y = jnp.dot(mix, wo_ref[0], preferred_element_type=F32)
    out = x_ref[...] + gate_ref[0] * y
    if final:
        fg_ref, o_ref = rest
        ms = jnp.mean(out * out, axis=-1, keepdims=True)
        o_ref[...] = out * lax.rsqrt(ms + EPS) * fg_ref[...]
    else:
        shift_ref, scale_ref, g_ref, o_ref, h_ref = rest
        o_ref[...] = out
        h_ref[...] = _modnorm(out, g_ref[0], scale_ref[0], shift_ref[0]).astype(BF16)


def _merge(rest, attn, fm, x2d, mods, layer, mod_row_of_tile, w_pa, w_pb, w_o, norm_g, final_g,
           tm):
    t = x2d.shape[0]
    final = layer == DEPTH - 1

    def rows(width, block=0):
        return pl.BlockSpec((tm, width), lambda i: (i, block))

    def weight(k):
        return pl.BlockSpec((1, k, D_MODEL), lambda i: (layer, 0, 0), pipeline_mode=pl.Buffered(1))

    in_specs = [
        rows(ATT_W), rows(ATT_W, REST_ZA_BLOCK), rows(FNET_W), rows(FNET_W, REST_ZB_BLOCK),
        rows(D_MODEL, REST_GA_BLOCK), rows(D_MODEL, REST_GB_BLOCK),
        rows(D_MODEL),
    ] + _mod_specs(layer, mod_row_of_tile, (MOD_GATE,)) + [
        weight(ATT_W), weight(FNET_W), weight(D_MODEL)]
    args = [attn, rest, fm, rest, rest, rest, x2d, mods, w_pa, w_pb, w_o]
    stream = jax.ShapeDtypeStruct((t, D_MODEL), F32)
    if final:
        in_specs += [pl.BlockSpec((1, D_MODEL), lambda i: (0, 0))]
        args += [final_g]
        out_specs, out_shape = rows(D_MODEL), stream
    else:
        in_specs += _mod_specs(layer + 1, mod_row_of_tile, (MOD_SHIFT, MOD_SCALE))
        in_specs += [pl.BlockSpec((1, 1, D_MODEL), lambda i: (layer + 1, 0, 0))]
        args += [mods, mods, norm_g]
        out_specs = [rows(D_MODEL), rows(D_MODEL)]
        out_shape = [stream, jax.ShapeDtypeStruct((t, D_MODEL), BF16)]
    return pl.pallas_call(
        functools.partial(_merge_kernel, final),
        grid=(t // tm,),
        in_specs=in_specs,
        out_specs=out_specs,
        out_shape=out_shape,
        compiler_params=pltpu.CompilerParams(
            dimension_semantics=("arbitrary",), vmem_limit_bytes=VMEM_LIMIT),
        name="merge",
    )(*args)


def _rope_tables(n_tokens):
    t = np.arange(n_tokens)
    pos = np.stack([t // GRID_W, t % GRID_W], axis=1).astype(np.float32)
    inv = (np.float32(ROPE_THETA) ** (-np.arange(ROPE_NFREQ, dtype=np.float32) / ROPE_NFREQ))
    ang = (pos[:, :, None] * inv[None, None, :]).astype(np.float32).astype(np.float64)
    cos, sin = np.cos(ang), np.sin(ang)
    zero = np.zeros_like(sin)
    c = np.stack([cos, cos], axis=2).reshape(n_tokens, HEAD_DIM)
    s1 = np.stack([-sin, zero], axis=2).reshape(n_tokens, HEAD_DIM)
    s2 = np.stack([zero, sin], axis=2).reshape(n_tokens, HEAD_DIM)
    return tuple(jnp.asarray(a.astype(np.float32)) for a in (c, s1, s2))


def _identity_tables(n_tokens):
    one = jnp.ones((n_tokens, HEAD_DIM), F32)
    zero = jnp.zeros((n_tokens, HEAD_DIM), F32)
    return one, zero, zero


def kernel(x, c, ctx, c_ctx, w_ada, b_ada, norm_g, w_in, q_norm_g, k_norm_g,
           w_proj_a, w_proj_b, w_out, final_g):
    batch, n_lat, d = x.shape
    n_ctx = ctx.shape[1]
    tm_norm, tm_merge, sub = 512, 256, 1024
    ctx_row = batch

    cvec = jnp.concatenate(
        [c, c_ctx[None], jnp.zeros((MOD_ROWS - batch - 1, d), F32)], axis=0)
    mods = _ada(cvec, w_ada, b_ada).reshape(DEPTH * MOD_ROWS * 3, 1, d)

    rope = _rope_tables(n_lat)
    no_rope = _identity_tables(batch * n_ctx)
    heads_per_tile = COL_TILE // HEAD_DIM
    qg = jnp.tile(q_norm_g * Q_SCALE, (1, heads_per_tile)).reshape(DEPTH, 1, COL_TILE)
    kg = jnp.tile(k_norm_g, (1, heads_per_tile)).reshape(DEPTH, 1, COL_TILE)
    w_pa, w_pb, w_o = (w.astype(BF16) for w in (w_proj_a, w_proj_b, w_out))
    norm_g3 = norm_g.reshape(DEPTH, 1, d)
    fg = final_g.reshape(1, d)

    def lat_row(tm):
        return lambda i: i // (n_lat // tm)

    def ctx_mod_row(i):
        return ctx_row

    xs = x.reshape(batch * n_lat, d)
    cs = ctx.reshape(batch * n_ctx, d)
    h = _modnorm_call(xs, mods, 0, lat_row(tm_norm), norm_g3, tm_norm)
    hc = _modnorm_call(cs, mods, 0, ctx_mod_row, norm_g3, tm_norm)
    for l in range(DEPTH):
        last = l == DEPTH - 1
        lo, hi = (K_TILE, V_TILE + 1) if last else (0, N_COL_TILES)
        pc = _in_proj(hc, w_in, l, qg, kg, no_rope, lo, hi, batch * n_ctx, sub)
        p = _in_proj(h, w_in, l, qg, kg, rope, 0, N_COL_TILES, n_lat, sub)

        attn = _attention(p["q"], [(p["k"], p["vt"]), (pc["k"], pc["vt"])], batch, 256, 512)
        fm = _fourier(p["rest"], batch, 512)
        merged = _merge(p["rest"], attn, fm, xs, mods, l, lat_row(tm_merge), w_pa, w_pb, w_o,
                        norm_g3, fg, tm_merge)
        if last:
            xs = merged
        else:
            xs, h = merged
            attn_c = _attention(pc["q"], [(pc["k"], pc["vt"])], batch, n_ctx, n_ctx)
            fm_c = _fourier(pc["rest"], batch, n_ctx)
            cs, hc = _merge(pc["rest"], attn_c, fm_c, cs, mods, l, ctx_mod_row, w_pa, w_pb, w_o,
                            norm_g3, fg, tm_merge)
    return xs.reshape(batch, n_lat, d)
```

```python
import functools
import math

import numpy as np
import jax
import jax.numpy as jnp
from jax import lax
from jax.experimental import pallas as pl
from jax.experimental.pallas import tpu as pltpu

F32 = jnp.float32
BF16 = jnp.bfloat16

D_MODEL = 2048
DEPTH = 2
CTX_LEN = 256
GRID_W = 64
HEAD_DIM = 128
ATT_W = (3 * D_MODEL) // 4
N_Q_HEADS = ATT_W // HEAD_DIM
N_KV_HEADS = 4
Q_PER_KV = N_Q_HEADS // N_KV_HEADS
KV_W = N_KV_HEADS * HEAD_DIM
FNET_W = D_MODEL // 4
N_FNET_GROUPS = 4
FNET_GROUP = FNET_W // N_FNET_GROUPS
ROPE_THETA = 10000.0
ROPE_NFREQ = HEAD_DIM // 4
EPS = 1e-6
IN_W = 2 * ATT_W + 2 * KV_W + 2 * FNET_W + 2 * D_MODEL
Q_SCALE = HEAD_DIM ** -0.5 * math.log2(math.e)

V7X_VMEM_BYTES = 64 * 1024 * 1024
VMEM_LIMIT = V7X_VMEM_BYTES - 8 * 1024 * 1024

COL_TILE = 512
N_COL_TILES = IN_W // COL_TILE
K_TILE = ATT_W // COL_TILE
V_TILE = K_TILE + 1
ZB_TILE = (2 * ATT_W + 2 * KV_W + FNET_W) // COL_TILE
REST_W = IN_W - ATT_W - 2 * KV_W
REST_ZA_BLOCK = 0
REST_UB_BLOCK = ATT_W // FNET_W
REST_GA_BLOCK = 1
REST_GB_BLOCK = 2
REST_ZB_BLOCK = REST_W // FNET_W - 1
MOD_ROWS = 8


def _silu(x):
    return x * jax.nn.sigmoid(x)


def _ada_kernel(cv_ref, w_ref, b_ref, o_ref):
    s = _silu(cv_ref[...]).astype(BF16)
    o_ref[0] = jnp.dot(s, w_ref[0].astype(BF16), preferred_element_type=F32) + b_ref[0]


def _ada(cvec, w_ada, b_ada):
    tn = 1024
    n = 3 * D_MODEL
    return pl.pallas_call(
        _ada_kernel,
        grid=(DEPTH, n // tn),
        in_specs=[
            pl.BlockSpec((MOD_ROWS, D_MODEL), lambda l, j: (0, 0)),
            pl.BlockSpec((1, D_MODEL, tn), lambda l, j: (l, 0, j)),
            pl.BlockSpec((1, 1, tn), lambda l, j: (l, 0, j)),
        ],
        out_specs=pl.BlockSpec((1, MOD_ROWS, tn), lambda l, j: (l, 0, j)),
        out_shape=jax.ShapeDtypeStruct((DEPTH, MOD_ROWS, n), F32),
        compiler_params=pltpu.CompilerParams(
            dimension_semantics=("arbitrary", "arbitrary"), vmem_limit_bytes=VMEM_LIMIT),
        name="ada",
    )(cvec, w_ada, b_ada.reshape(DEPTH, 1, n))


def _weight_tile(step):
    return jnp.where(step < ZB_TILE, step, jnp.where(step < N_COL_TILES - 1, step + 1, ZB_TILE))


def _head_norm_rope(acc, head_mean, g, c, s1, s2):
    ms = jnp.dot((acc * acc).astype(BF16), head_mean, preferred_element_type=F32)
    y = acc * lax.rsqrt(ms + EPS) * g
    outs = []
    for hh in range(COL_TILE // HEAD_DIM):
        yh = y[:, hh * HEAD_DIM:(hh + 1) * HEAD_DIM]
        outs.append(yh * c + pltpu.roll(yh, HEAD_DIM - ROPE_NFREQ, 1) * s1
                    + pltpu.roll(yh, ROPE_NFREQ, 1) * s2)
    return jnp.concatenate(outs, axis=-1)


def _modnorm(xf, g, scale, shift):
    ms = jnp.mean(xf * xf, axis=-1, keepdims=True)
    return xf * lax.rsqrt(ms + EPS) * g * (1.0 + scale) + shift


def _modnorm_kernel(x_ref, shift_ref, scale_ref, g_ref, h_ref):
    h_ref[...] = _modnorm(x_ref[...], g_ref[0], scale_ref[0], shift_ref[0]).astype(BF16)


def _mod_specs(layer, mod_row_of_tile, parts):
    return [pl.BlockSpec((1, 1, D_MODEL),
                         lambda i, p=p: ((layer * MOD_ROWS + mod_row_of_tile(i)) * 3 + p, 0, 0))
            for p in parts]


MOD_SHIFT, MOD_SCALE, MOD_GATE = 0, 1, 2


def _modnorm_call(x2d, mods, layer, mod_row_of_tile, norm_g, tm):
    t = x2d.shape[0]
    return pl.pallas_call(
        _modnorm_kernel,
        grid=(t // tm,),
        in_specs=[pl.BlockSpec((tm, D_MODEL), lambda i: (i, 0))]
        + _mod_specs(layer, mod_row_of_tile, (MOD_SHIFT, MOD_SCALE))
        + [pl.BlockSpec((1, 1, D_MODEL), lambda i: (layer, 0, 0))],
        out_specs=pl.BlockSpec((tm, D_MODEL), lambda i: (i, 0)),
        out_shape=jax.ShapeDtypeStruct((t, D_MODEL), BF16),
        compiler_params=pltpu.CompilerParams(
            dimension_semantics=("arbitrary",), vmem_limit_bytes=VMEM_LIMIT),
        name="modnorm",
    )(x2d, mods, mods, norm_g)


def _in_proj_kernel(step_lo, names, sub, h_ref, w_ref, qg_ref, kg_ref, hm_ref,
                    c_ref, s1_ref, s2_ref, *out_refs):
    out = dict(zip(names, out_refs))
    step = pl.program_id(1) + step_lo
    row_tiles = [slice(r, r + sub) for r in range(0, h_ref.shape[0], sub)]

    def tiles():
        w = w_ref[0].astype(BF16)
        for rows in row_tiles:
            yield rows, jnp.dot(h_ref[rows, :], w, preferred_element_type=F32)

    def head_tiles(o_ref, g_ref):
        for rows, acc in tiles():
            o_ref[rows, :] = _head_norm_rope(
                acc, hm_ref[...], g_ref[0], c_ref[rows, :], s1_ref[rows, :],
                s2_ref[rows, :]).astype(BF16)

    if "q" in out:
        @pl.when(step < K_TILE)
        def _():
            head_tiles(out["q"], qg_ref)

    if "k" in out:
        @pl.when(step == K_TILE)
        def _():
            head_tiles(out["k"], kg_ref)

    if "vt" in out:
        @pl.when(step == V_TILE)
        def _():
            for rows, acc in tiles():
                out["vt"][:, rows] = acc.T.astype(BF16)

    if "rest" in out:
        @pl.when(step > V_TILE)
        def _():
            for rows, acc in tiles():
                out["rest"][rows, :] = acc.astype(BF16)


def _in_proj(h, w_in, layer, q_g, k_g, tables, step_lo, step_hi, group, sub):
    t = h.shape[0]
    c_tab, s1_tab, s2_tab = tables
    n_rest = REST_W // COL_TILE
    heads_per_tile = COL_TILE // HEAD_DIM
    head_mean = jnp.asarray(
        np.kron(np.eye(heads_per_tile), np.full((HEAD_DIM, HEAD_DIM), 1.0 / HEAD_DIM)), BF16)
    specs = {
        "q": (pl.BlockSpec((group, COL_TILE),
                           lambda i, j: (i, jnp.clip(j + step_lo, 0, K_TILE - 1))),
              jax.ShapeDtypeStruct((t, ATT_W), BF16), step_lo < K_TILE),
        "k": (pl.BlockSpec((group, COL_TILE), lambda i, j: (i, 0)),
              jax.ShapeDtypeStruct((t, KV_W), BF16), step_lo <= K_TILE < step_hi),
        "vt": (pl.BlockSpec((COL_TILE, group), lambda i, j: (0, i)),
               jax.ShapeDtypeStruct((KV_W, t), BF16), step_lo <= V_TILE < step_hi),
        "rest": (pl.BlockSpec((group, COL_TILE),
                              lambda i, j: (i, jnp.clip(j + step_lo - V_TILE - 1, 0, n_rest - 1))),
                 jax.ShapeDtypeStruct((t, REST_W), BF16), step_hi > V_TILE + 1),
    }
    names = tuple(n for n, (_, _, present) in specs.items() if present)
    tab_spec = pl.BlockSpec((group, HEAD_DIM), lambda i, j: (0, 0))
    gain_spec = pl.BlockSpec((1, 1, COL_TILE), lambda i, j: (layer, 0, 0))
    outs = pl.pallas_call(
        functools.partial(_in_proj_kernel, step_lo, names, sub),
        grid=(t // group, step_hi - step_lo),
        in_specs=[
            pl.BlockSpec((group, D_MODEL), lambda i, j: (i, 0), pipeline_mode=pl.Buffered(1)),
            pl.BlockSpec((1, D_MODEL, COL_TILE),
                         lambda i, j: (layer, 0, _weight_tile(j + step_lo))),
            gain_spec, gain_spec,
            pl.BlockSpec((COL_TILE, COL_TILE), lambda i, j: (0, 0)),
            tab_spec, tab_spec, tab_spec,
        ],
        out_specs=[specs[n][0] for n in names],
        out_shape=[specs[n][1] for n in names],
        compiler_params=pltpu.CompilerParams(
            dimension_semantics=("arbitrary", "arbitrary"), vmem_limit_bytes=VMEM_LIMIT),
        name="in_proj",
    )(h, w_in, q_g, k_g, head_mean, c_tab, s1_tab, s2_tab)
    return dict(zip(names, outs))


ONES_ROWS = 16


def _col_reduce8(x, op):
    rows, w = x.shape
    return op(x.reshape(rows // 8, 8, w), axis=0)


def _attn_kernel(chunks, tq, q_ref, *rest):
    n_sets = 1 + max(c[0] for c in chunks)
    k_refs, vt_refs = rest[:n_sets], rest[n_sets:2 * n_sets]
    o_ref, s_scr, m_scr = rest[2 * n_sets:]
    width = Q_PER_KV * tq

    @pl.when(pl.program_id(0) == 0)
    def _():
        s_scr[...] = jnp.zeros_like(s_scr)
        m_scr[...] = jnp.zeros_like(m_scr)

    q3 = q_ref[...]
    qs = jnp.concatenate(
        [q3[:, i * HEAD_DIM:(i + 1) * HEAD_DIM] for i in range(Q_PER_KV)], axis=0)
    m_prev = m_scr[...]
    m_run = jnp.full((8, width), -jnp.inf, F32)
    acc = jnp.zeros((HEAD_DIM + ONES_ROWS, width), F32)
    for t, r0, rows, s0 in chunks:
        p = jnp.exp2(s_scr[s0:s0 + rows, :] - m_prev).astype(BF16)
        vt_ones = jnp.concatenate(
            [vt_refs[t][:, r0:r0 + rows], jnp.ones((ONES_ROWS, rows), BF16)], axis=0)
        acc = acc + jnp.dot(vt_ones, p, preferred_element_type=F32)
        s_new = lax.dot_general(k_refs[t][r0:r0 + rows, :], qs, (((1,), (1,)), ((), ())),
                                preferred_element_type=F32)
        s_scr[s0:s0 + rows, :] = s_new
        m_run = jnp.maximum(m_run, _col_reduce8(s_new, jnp.max))
    m_scr[...] = m_run.max(axis=0, keepdims=True)
    o_t = acc[:HEAD_DIM] / acc[HEAD_DIM:HEAD_DIM + 1]
    for i in range(Q_PER_KV):
        o_ref[:, i * HEAD_DIM:(i + 1) * HEAD_DIM] = o_t[:, i * tq:(i + 1) * tq].T.astype(BF16)


def _attention(q, kv_sets, batch, tq, key_chunk):
    t = q.shape[0]
    nq = t // batch // tq
    n_tiles = batch * N_KV_HEADS * nq
    gw = Q_PER_KV * HEAD_DIM

    def tile(n):
        return n // (N_KV_HEADS * nq), (n // nq) % N_KV_HEADS, n % nq

    def score_tile(n):
        return tile(jnp.minimum(n, n_tiles - 1))

    def out_tile(n):
        return tile(jnp.maximum(n - 1, 0))

    def q_map(n):
        b, g, i = score_tile(n)
        return b * nq + i, g

    def o_map(n):
        b, g, i = out_tile(n)
        return b * nq + i, g

    def k_map(n):
        b, g, _ = score_tile(n)
        return b, g

    def vt_map(n):
        b, g, _ = out_tile(n)
        return g, b

    chunks, k_specs, vt_specs, s_rows = [], [], [], 0
    for s, (k, _) in enumerate(kv_sets):
        m_t = k.shape[0] // batch
        step = min(key_chunk, m_t)
        for r0 in range(0, m_t, step):
            chunks.append((s, r0, step, s_rows + r0))
        s_rows += m_t
        k_specs.append(pl.BlockSpec((m_t, HEAD_DIM), k_map))
        vt_specs.append(pl.BlockSpec((HEAD_DIM, m_t), vt_map))
    return pl.pallas_call(
        functools.partial(_attn_kernel, tuple(chunks), tq),
        grid=(n_tiles + 1,),
        in_specs=[pl.BlockSpec((tq, gw), q_map)] + k_specs + vt_specs,
        out_specs=pl.BlockSpec((tq, gw), o_map),
        out_shape=jax.ShapeDtypeStruct((t, ATT_W), BF16),
        scratch_shapes=[pltpu.VMEM((s_rows, Q_PER_KV * tq), F32),
                        pltpu.VMEM((1, Q_PER_KV * tq), F32)],
        compiler_params=pltpu.CompilerParams(
            dimension_semantics=("arbitrary",), vmem_limit_bytes=VMEM_LIMIT),
        name="attn",
    )(q, *[k for k, _ in kv_sets], *[vt for _, vt in kv_sets])


def _dft_tables(n):
    def cs(m):
        idx = np.arange(m, dtype=np.int64)
        ang = 2.0 * np.pi * ((idx[:, None] * idx[None, :]) % m).astype(np.float64) / m
        return np.cos(ang) / np.sqrt(m), np.sin(ang) / np.sqrt(m)

    cn, sn = cs(n)
    cc, sc = cs(FNET_GROUP)
    return (np.concatenate([cn, -sn], axis=1).astype(np.float32),
            cc.astype(np.float32), sc.astype(np.float32))


def _fourier_kernel(n, u_ref, w_ref, cc_ref, sc_ref, o_ref, x_ref):
    @pl.when(pl.program_id(1) == 0)
    def _():
        for g in range(N_FNET_GROUPS):
            cols = slice(g * FNET_GROUP, (g + 1) * FNET_GROUP)
            ug = u_ref[:, cols]
            x_ref[0:n, cols] = jnp.dot(ug, cc_ref[...], preferred_element_type=F32).astype(BF16)
            x_ref[n:2 * n, cols] = jnp.dot(ug, sc_ref[...], preferred_element_type=F32).astype(BF16)

    o_ref[...] = jnp.dot(w_ref[...], x_ref[...], preferred_element_type=F32).astype(BF16)


def _fourier(rest, batch, tr):
    t = rest.shape[0]
    n = t // batch
    w, cc, sc = (jnp.asarray(a).astype(BF16) for a in _dft_tables(n))
    nr = n // tr
    return pl.pallas_call(
        functools.partial(_fourier_kernel, n),
        grid=(batch, nr),
        in_specs=[
            pl.BlockSpec((n, FNET_W), lambda b, r: (b, REST_UB_BLOCK)),
            pl.BlockSpec((tr, 2 * n), lambda b, r: (r, 0)),
            pl.BlockSpec((FNET_GROUP, FNET_GROUP), lambda b, r: (0, 0)),
            pl.BlockSpec((FNET_GROUP, FNET_GROUP), lambda b, r: (0, 0)),
        ],
        out_specs=pl.BlockSpec((tr, FNET_W), lambda b, r: (b * nr + r, 0)),
        out_shape=jax.ShapeDtypeStruct((t, FNET_W), BF16),
        scratch_shapes=[pltpu.VMEM((2 * n, FNET_W), BF16)],
        compiler_params=pltpu.CompilerParams(
            dimension_semantics=("arbitrary", "arbitrary"), vmem_limit_bytes=VMEM_LIMIT),
        name="fourier",
    )(rest, w, cc, sc)


def _merge_kernel(final, attn_ref, za_ref, fm_ref, zb_ref, ga_ref, gb_ref, x_ref, gate_ref,
                  wpa_ref, wpb_ref, wo_ref, *rest):
    ta = (attn_ref[...].astype(F32) * _silu(za_ref[...].astype(F32))).astype(BF16)
    ya = jnp.dot(ta, wpa_ref[0], preferred_element_type=F32)
    tb = (fm_ref[...].astype(F32) * _silu(zb_ref[...].astype(F32))).astype(BF16)
    yb = jnp.dot(tb, wpb_ref[0], preferred_element_type=F32)
    mix = (jax.nn.sigmoid(ga_ref[...].astype(F32)) * ya
           + jax.nn.sigmoid(gb_ref[...].astype(F32)) * yb).astype(BF16)
    y = jnp.dot(mix, wo_ref[0], preferred_element_type=F32)
    out = x_ref[...] + gate_ref[0] * y
    if final:
        fg_ref, o_ref = rest
        ms = jnp.mean(out * out, axis=-1, keepdims=True)
        o_ref[...] = out * lax.rsqrt(ms + EPS) * fg_ref[...]
    else:
        shift_ref, scale_ref, g_ref, o_ref, h_ref = rest
        o_ref[...] = out
        h_ref[...] = _modnorm(out, g_ref[0], scale_ref[0], shift_ref[0]).astype(BF16)


def _merge(rest, attn, fm, x2d, mods, layer, mod_row_of_tile, w_pa, w_pb, w_o, norm_g, final_g,
           tm):
    t = x2d.shape[0]
    final = layer == DEPTH - 1

    def rows(width, block=0):
        return pl.BlockSpec((tm, width), lambda i: (i, block))

    def weight(k):
        return pl.BlockSpec((1, k, D_MODEL), lambda i: (layer, 0, 0), pipeline_mode=pl.Buffered(1))

    in_specs = [
        rows(ATT_W), rows(ATT_W, REST_ZA_BLOCK), rows(FNET_W), rows(FNET_W, REST_ZB_BLOCK),
        rows(D_MODEL, REST_GA_BLOCK), rows(D_MODEL, REST_GB_BLOCK),
        rows(D_MODEL),
    ] + _mod_specs(layer, mod_row_of_tile, (MOD_GATE,)) + [
        weight(ATT_W), weight(FNET_W), weight(D_MODEL)]
    args = [attn, rest, fm, rest, rest, rest, x2d, mods, w_pa, w_pb, w_o]
    stream = jax.ShapeDtypeStruct((t, D_MODEL), F32)
    if final:
        in_specs += [pl.BlockSpec((1, D_MODEL), lambda i: (0, 0))]
        args += [final_g]
        out_specs, out_shape = rows(D_MODEL), stream
    else:
        in_specs += _mod_specs(layer + 1, mod_row_of_tile, (MOD_SHIFT, MOD_SCALE))
        in_specs += [pl.BlockSpec((1, 1, D_MODEL), lambda i: (layer + 1, 0, 0))]
        args += [mods, mods, norm_g]
        out_specs = [rows(D_MODEL), rows(D_MODEL)]
        out_shape = [stream, jax.ShapeDtypeStruct((t, D_MODEL), BF16)]
    return pl.pallas_call(
        functools.partial(_merge_kernel, final),
        grid=(t // tm,),
        in_specs=in_specs,
        out_specs=out_specs,
        out_shape=out_shape,
        compiler_params=pltpu.CompilerParams(
            dimension_semantics=("arbitrary",), vmem_limit_bytes=VMEM_LIMIT),
        name="merge",
    )(*args)


def _rope_tables(n_tokens):
    t = np.arange(n_tokens)
    pos = np.stack([t // GRID_W, t % GRID_W], axis=1).astype(np.float32)
    inv = (np.float32(ROPE_THETA) ** (-np.arange(ROPE_NFREQ, dtype=np.float32) / ROPE_NFREQ))
    ang = (pos[:, :, None] * inv[None, None, :]).astype(np.float32).astype(np.float64)
    cos, sin = np.cos(ang), np.sin(ang)
    zero = np.zeros_like(sin)
    c = np.stack([cos, cos], axis=2).reshape(n_tokens, HEAD_DIM)
    s1 = np.stack([-sin, zero], axis=2).reshape(n_tokens, HEAD_DIM)
    s2 = np.stack([zero, sin], axis=2).reshape(n_tokens, HEAD_DIM)
    return tuple(jnp.asarray(a.astype(np.float32)) for a in (c, s1, s2))


def _identity_tables(n_tokens):
    one = jnp.ones((n_tokens, HEAD_DIM), F32)
    zero = jnp.zeros((n_tokens, HEAD_DIM), F32)
    return one, zero, zero


def kernel(x, c, ctx, c_ctx, w_ada, b_ada, norm_g, w_in, q_norm_g, k_norm_g,
           w_proj_a, w_proj_b, w_out, final_g):
    batch, n_lat, d = x.shape
    n_ctx = ctx.shape[1]
    tm_norm, tm_merge, sub = 512, 256, 1024
    ctx_row = batch

    cvec = jnp.concatenate(
        [c, c_ctx[None], jnp.zeros((MOD_ROWS - batch - 1, d), F32)], axis=0)
    mods = _ada(cvec, w_ada, b_ada).reshape(DEPTH * MOD_ROWS * 3, 1, d)

    rope = _rope_tables(n_lat)
    no_rope = _identity_tables(batch * n_ctx)
    heads_per_tile = COL_TILE // HEAD_DIM
    qg = jnp.tile(q_norm_g * Q_SCALE, (1, heads_per_tile)).reshape(DEPTH, 1, COL_TILE)
    kg = jnp.tile(k_norm_g, (1, heads_per_tile)).reshape(DEPTH, 1, COL_TILE)
    w_pa, w_pb, w_o = (w.astype(BF16) for w in (w_proj_a, w_proj_b, w_out))
    norm_g3 = norm_g.reshape(DEPTH, 1, d)
    fg = final_g.reshape(1, d)

    def lat_row(tm):
        return lambda i: i // (n_lat // tm)

    def ctx_mod_row(i):
        return ctx_row

    xs = x.reshape(batch * n_lat, d)
    cs = ctx.reshape(batch * n_ctx, d)
    h = _modnorm_call(xs, mods, 0, lat_row(tm_norm), norm_g3, tm_norm)
    hc = _modnorm_call(cs, mods, 0, ctx_mod_row, norm_g3, tm_norm)
    for l in range(DEPTH):
        last = l == DEPTH - 1
        lo, hi = (K_TILE, V_TILE + 1) if last else (0, N_COL_TILES)
        pc = _in_proj(hc, w_in, l, qg, kg, no_rope, lo, hi, batch * n_ctx, sub)
        p = _in_proj(h, w_in, l, qg, kg, rope, 0, N_COL_TILES, n_lat, sub)

        attn = _attention(p["q"], [(p["k"], p["vt"]), (pc["k"], pc["vt"])], batch, 512, 512)
        fm = _fourier(p["rest"], batch, 512)
        merged = _merge(p["rest"], attn, fm, xs, mods, l, lat_row(tm_merge), w_pa, w_pb, w_o,
                        norm_g3, fg, tm_merge)
        if last:
            xs = merged
        else:
            xs, h = merged
            attn_c = _attention(pc["q"], [(pc["k"], pc["vt"])], batch, n_ctx, n_ctx)
            fm_c = _fourier(pc["rest"], batch, n_ctx)
            cs, hc = _merge(pc["rest"], attn_c, fm_c, cs, mods, l, ctx_mod_row, w_pa, w_pb, w_o,
                            norm_g3, fg, tm_merge)
    return xs.reshape(batch, n_lat, d)
```

```python
import functools
import math

import numpy as np
import jax
import jax.numpy as jnp
from jax import lax
from jax.experimental import pallas as pl
from jax.experimental.pallas import tpu as pltpu

F32 = jnp.float32
BF16 = jnp.bfloat16

D_MODEL = 2048
DEPTH = 2
CTX_LEN = 256
GRID_W = 64
HEAD_DIM = 128
ATT_W = (3 * D_MODEL) // 4
N_Q_HEADS = ATT_W // HEAD_DIM
N_KV_HEADS = 4
Q_PER_KV = N_Q_HEADS // N_KV_HEADS
KV_W = N_KV_HEADS * HEAD_DIM
FNET_W = D_MODEL // 4
N_FNET_GROUPS = 4
FNET_GROUP = FNET_W // N_FNET_GROUPS
ROPE_THETA = 10000.0
ROPE_NFREQ = HEAD_DIM // 4
EPS = 1e-6
IN_W = 2 * ATT_W + 2 * KV_W + 2 * FNET_W + 2 * D_MODEL
Q_SCALE = HEAD_DIM ** -0.5 * math.log2(math.e)

V7X_VMEM_BYTES = 64 * 1024 * 1024
VMEM_LIMIT = V7X_VMEM_BYTES - 8 * 1024 * 1024

COL_TILE = 512
N_COL_TILES = IN_W // COL_TILE
K_TILE = ATT_W // COL_TILE
V_TILE = K_TILE + 1
ZB_TILE = (2 * ATT_W + 2 * KV_W + FNET_W) // COL_TILE
REST_W = IN_W - ATT_W - 2 * KV_W
REST_ZA_BLOCK = 0
REST_UB_BLOCK = ATT_W // FNET_W
REST_GA_BLOCK = 1
REST_GB_BLOCK = 2
REST_ZB_BLOCK = REST_W // FNET_W - 1
MOD_ROWS = 8


def _silu(x):
    return x * jax.nn.sigmoid(x)


def _ada_kernel(cv_ref, w_ref, b_ref, o_ref):
    s = _silu(cv_ref[...]).astype(BF16)
    o_ref[0] = jnp.dot(s, w_ref[0].astype(BF16), preferred_element_type=F32) + b_ref[0]


def _ada(cvec, w_ada, b_ada):
    tn = 1024
    n = 3 * D_MODEL
    return pl.pallas_call(
        _ada_kernel,
        grid=(DEPTH, n // tn),
        in_specs=[
            pl.BlockSpec((MOD_ROWS, D_MODEL), lambda l, j: (0, 0)),
            pl.BlockSpec((1, D_MODEL, tn), lambda l, j: (l, 0, j)),
            pl.BlockSpec((1, 1, tn), lambda l, j: (l, 0, j)),
        ],
        out_specs=pl.BlockSpec((1, MOD_ROWS, tn), lambda l, j: (l, 0, j)),
        out_shape=jax.ShapeDtypeStruct((DEPTH, MOD_ROWS, n), F32),
        compiler_params=pltpu.CompilerParams(
            dimension_semantics=("arbitrary", "arbitrary"), vmem_limit_bytes=VMEM_LIMIT),
        name="ada",
    )(cvec, w_ada, b_ada.reshape(DEPTH, 1, n))


def _weight_tile(step):
    return jnp.where(step < ZB_TILE, step, jnp.where(step < N_COL_TILES - 1, step + 1, ZB_TILE))


def _head_norm_rope(acc, head_mean, g, c, s1, s2):
    ms = jnp.dot((acc * acc).astype(BF16), head_mean, preferred_element_type=F32)
    y = acc * lax.rsqrt(ms + EPS) * g
    outs = []
    for hh in range(COL_TILE // HEAD_DIM):
        yh = y[:, hh * HEAD_DIM:(hh + 1) * HEAD_DIM]
        outs.append(yh * c + pltpu.roll(yh, HEAD_DIM - ROPE_NFREQ, 1) * s1
                    + pltpu.roll(yh, ROPE_NFREQ, 1) * s2)
    return jnp.concatenate(outs, axis=-1)


def _modnorm(xf, g, scale, shift):
    ms = jnp.mean(xf * xf, axis=-1, keepdims=True)
    return xf * lax.rsqrt(ms + EPS) * g * (1.0 + scale) + shift


def _modnorm_kernel(x_ref, shift_ref, scale_ref, g_ref, h_ref):
    h_ref[...] = _modnorm(x_ref[...], g_ref[0], scale_ref[0], shift_ref[0]).astype(BF16)


def _mod_specs(layer, mod_row_of_tile, parts):
    return [pl.BlockSpec((1, 1, D_MODEL),
                         lambda i, p=p: ((layer * MOD_ROWS + mod_row_of_tile(i)) * 3 + p, 0, 0))
            for p in parts]


MOD_SHIFT, MOD_SCALE, MOD_GATE = 0, 1, 2


def _modnorm_call(x2d, mods, layer, mod_row_of_tile, norm_g, tm):
    t = x2d.shape[0]
    return pl.pallas_call(
        _modnorm_kernel,
        grid=(t // tm,),
        in_specs=[pl.BlockSpec((tm, D_MODEL), lambda i: (i, 0))]
        + _mod_specs(layer, mod_row_of_tile, (MOD_SHIFT, MOD_SCALE))
        + [pl.BlockSpec((1, 1, D_MODEL), lambda i: (layer, 0, 0))],
        out_specs=pl.BlockSpec((tm, D_MODEL), lambda i: (i, 0)),
        out_shape=jax.ShapeDtypeStruct((t, D_MODEL), BF16),
        compiler_params=pltpu.CompilerParams(
            dimension_semantics=("arbitrary",), vmem_limit_bytes=VMEM_LIMIT),
        name="modnorm",
    )(x2d, mods, mods, norm_g)


def _in_proj_kernel(step_lo, names, sub, h_ref, w_ref, qg_ref, kg_ref, hm_ref,
                    c_ref, s1_ref, s2_ref, *out_refs):
    out = dict(zip(names, out_refs))
    step = pl.program_id(1) + step_lo
    row_tiles = [slice(r, r + sub) for r in range(0, h_ref.shape[0], sub)]

    def tiles():
        w = w_ref[0].astype(BF16)
        for rows in row_tiles:
            yield rows, jnp.dot(h_ref[rows, :], w, preferred_element_type=F32)

    def head_tiles(o_ref, g_ref):
        for rows, acc in tiles():
            o_ref[rows, :] = _head_norm_rope(
                acc, hm_ref[...], g_ref[0], c_ref[rows, :], s1_ref[rows, :],
                s2_ref[rows, :]).astype(BF16)

    if "q" in out:
        @pl.when(step < K_TILE)
        def _():
            head_tiles(out["q"], qg_ref)

    if "k" in out:
        @pl.when(step == K_TILE)
        def _():
            head_tiles(out["k"], kg_ref)

    if "vt" in out:
        @pl.when(step == V_TILE)
        def _():
            for rows, acc in tiles():
                out["vt"][:, rows] = acc.T.astype(BF16)

    if "rest" in out:
        @pl.when(step > V_TILE)
        def _():
            for rows, acc in tiles():
                out["rest"][rows, :] = acc.astype(BF16)


def _in_proj(h, w_in, layer, q_g, k_g, tables, step_lo, step_hi, group, sub):
    t = h.shape[0]
    c_tab, s1_tab, s2_tab = tables
    n_rest = REST_W // COL_TILE
    heads_per_tile = COL_TILE // HEAD_DIM
    head_mean = jnp.asarray(
        np.kron(np.eye(heads_per_tile), np.full((HEAD_DIM, HEAD_DIM), 1.0 / HEAD_DIM)), BF16)
    specs = {
        "q": (pl.BlockSpec((group, COL_TILE),
                           lambda i, j: (i, jnp.clip(j + step_lo, 0, K_TILE - 1))),
              jax.ShapeDtypeStruct((t, ATT_W), BF16), step_lo < K_TILE),
        "k": (pl.BlockSpec((group, COL_TILE), lambda i, j: (i, 0)),
              jax.ShapeDtypeStruct((t, KV_W), BF16), step_lo <= K_TILE < step_hi),
        "vt": (pl.BlockSpec((COL_TILE, group), lambda i, j: (0, i)),
               jax.ShapeDtypeStruct((KV_W, t), BF16), step_lo <= V_TILE < step_hi),
        "rest": (pl.BlockSpec((group, COL_TILE),
                              lambda i, j: (i, jnp.clip(j + step_lo - V_TILE - 1, 0, n_rest - 1))),
                 jax.ShapeDtypeStruct((t, REST_W), BF16), step_hi > V_TILE + 1),
    }
    names = tuple(n for n, (_, _, present) in specs.items() if present)
    tab_spec = pl.BlockSpec((group, HEAD_DIM), lambda i, j: (0, 0))
    gain_spec = pl.BlockSpec((1, 1, COL_TILE), lambda i, j: (layer, 0, 0))
    outs = pl.pallas_call(
        functools.partial(_in_proj_kernel, step_lo, names, sub),
        grid=(t // group, step_hi - step_lo),
        in_specs=[
            pl.BlockSpec((group, D_MODEL), lambda i, j: (i, 0)),
            pl.BlockSpec((1, D_MODEL, COL_TILE),
                         lambda i, j: (layer, 0, _weight_tile(j + step_lo))),
            gain_spec, gain_spec,
            pl.BlockSpec((COL_TILE, COL_TILE), lambda i, j: (0, 0)),
            tab_spec, tab_spec, tab_spec,
        ],
        out_specs=[specs[n][0] for n in names],
        out_shape=[specs[n][1] for n in names],
        compiler_params=pltpu.CompilerParams(
            dimension_semantics=("arbitrary", "arbitrary"), vmem_limit_bytes=VMEM_LIMIT),
        name="in_proj",
    )(h, w_in, q_g, k_g, head_mean, c_tab, s1_tab, s2_tab)
    return dict(zip(names, outs))


ONES_ROWS = 16


def _col_reduce8(x, op):
    rows, w = x.shape
    return op(x.reshape(rows // 8, 8, w), axis=0)


def _attn_kernel(chunks, tq, n_casts, q_ref, *rest):
    n_sets = 1 + max(c[0] for c in chunks)
    k_refs, vt_refs = rest[:n_sets], rest[n_sets:2 * n_sets]
    cast_in = rest[2 * n_sets:2 * n_sets + n_casts]
    o_ref = rest[2 * n_sets + n_casts]
    cast_out = rest[2 * n_sets + n_casts + 1:2 * n_sets + 2 * n_casts + 1]
    s_scr, m_scr = rest[2 * n_sets + 2 * n_casts + 1:]
    width = Q_PER_KV * tq
    for src, dst in zip(cast_in, cast_out):
        dst[...] = src[...].astype(BF16)

    @pl.when(pl.program_id(0) == 0)
    def _():
        s_scr[...] = jnp.zeros_like(s_scr)
        m_scr[...] = jnp.zeros_like(m_scr)

    q3 = q_ref[...]
    qs = jnp.concatenate(
        [q3[:, i * HEAD_DIM:(i + 1) * HEAD_DIM] for i in range(Q_PER_KV)], axis=0)
    q_t = qs.astype(F32).T.astype(BF16)
    m_prev = m_scr[...]
    m_run = jnp.full((8, width), -jnp.inf, F32)
    acc = jnp.zeros((HEAD_DIM + ONES_ROWS, width), F32)
    for t, r0, rows, s0 in chunks:
        p = jnp.exp2(s_scr[s0:s0 + rows, :] - m_prev).astype(BF16)
        vt_ones = jnp.concatenate(
            [vt_refs[t][:, r0:r0 + rows], jnp.ones((ONES_ROWS, rows), BF16)], axis=0)
        acc = acc + jnp.dot(vt_ones, p, preferred_element_type=F32)
        s_new = jnp.dot(k_refs[t][r0:r0 + rows, :], q_t, preferred_element_type=F32)
        s_scr[s0:s0 + rows, :] = s_new
        m_run = jnp.maximum(m_run, _col_reduce8(s_new, jnp.max))
    m_scr[...] = m_run.max(axis=0, keepdims=True)
    o_t = acc[:HEAD_DIM] / acc[HEAD_DIM:HEAD_DIM + 1]
    for i in range(Q_PER_KV):
        o_ref[:, i * HEAD_DIM:(i + 1) * HEAD_DIM] = o_t[:, i * tq:(i + 1) * tq].T.astype(BF16)


CAST_STEPS = 64


def _attention(q, kv_sets, batch, tq, key_chunk, casts=()):
    t = q.shape[0]
    nq = t // batch // tq
    n_tiles = batch * N_KV_HEADS * nq
    gw = Q_PER_KV * HEAD_DIM
    assert not casts or n_tiles + 1 >= CAST_STEPS
    cast_specs = [pl.BlockSpec((a.shape[0] // CAST_STEPS, a.shape[1]),
                               lambda n: (jnp.minimum(n, CAST_STEPS - 1), 0)) for a in casts]
    cast_shapes = [jax.ShapeDtypeStruct(a.shape, BF16) for a in casts]

    def tile(n):
        return n // (N_KV_HEADS * nq), (n // nq) % N_KV_HEADS, n % nq

    def score_tile(n):
        return tile(jnp.minimum(n, n_tiles - 1))

    def out_tile(n):
        return tile(jnp.maximum(n - 1, 0))

    def q_map(n):
        b, g, i = score_tile(n)
        return b * nq + i, g

    def o_map(n):
        b, g, i = out_tile(n)
        return b * nq + i, g

    def k_map(n):
        b, g, _ = score_tile(n)
        return b, g

    def vt_map(n):
        b, g, _ = out_tile(n)
        return g, b

    chunks, k_specs, vt_specs, s_rows = [], [], [], 0
    for s, (k, _) in enumerate(kv_sets):
        m_t = k.shape[0] // batch
        step = min(key_chunk, m_t)
        for r0 in range(0, m_t, step):
            chunks.append((s, r0, step, s_rows + r0))
        s_rows += m_t
        k_specs.append(pl.BlockSpec((m_t, HEAD_DIM), k_map))
        vt_specs.append(pl.BlockSpec((HEAD_DIM, m_t), vt_map))
    outs = pl.pallas_call(
        functools.partial(_attn_kernel, tuple(chunks), tq, len(casts)),
        grid=(n_tiles + 1,),
        in_specs=[pl.BlockSpec((tq, gw), q_map)] + k_specs + vt_specs + cast_specs,
        out_specs=[pl.BlockSpec((tq, gw), o_map)] + cast_specs,
        out_shape=[jax.ShapeDtypeStruct((t, ATT_W), BF16)] + cast_shapes,
        scratch_shapes=[pltpu.VMEM((s_rows, Q_PER_KV * tq), F32),
                        pltpu.VMEM((1, Q_PER_KV * tq), F32)],
        compiler_params=pltpu.CompilerParams(
            dimension_semantics=("arbitrary",), vmem_limit_bytes=VMEM_LIMIT),
        name="attn",
    )(q, *[k for k, _ in kv_sets], *[vt for _, vt in kv_sets], *casts)
    return outs if casts else outs[0]


def _dft_tables(n):
    def cs(m):
        idx = np.arange(m, dtype=np.int64)
        ang = 2.0 * np.pi * ((idx[:, None] * idx[None, :]) % m).astype(np.float64) / m
        return np.cos(ang) / np.sqrt(m), np.sin(ang) / np.sqrt(m)

    cn, sn = cs(n)
    cc, sc = cs(FNET_GROUP)
    return (np.concatenate([cn, -sn], axis=1).astype(np.float32),
            cc.astype(np.float32), sc.astype(np.float32))


def _fourier_kernel(n, u_ref, w_ref, cc_ref, sc_ref, o_ref, x_ref):
    @pl.when(pl.program_id(1) == 0)
    def _():
        for g in range(N_FNET_GROUPS):
            cols = slice(g * FNET_GROUP, (g + 1) * FNET_GROUP)
            ug = u_ref[:, cols]
            x_ref[0:n, cols] = jnp.dot(ug, cc_ref[...], preferred_element_type=F32).astype(BF16)
            x_ref[n:2 * n, cols] = jnp.dot(ug, sc_ref[...], preferred_element_type=F32).astype(BF16)

    o_ref[...] = jnp.dot(w_ref[...], x_ref[...], preferred_element_type=F32).astype(BF16)


def _fourier(rest, batch, tr):
    t = rest.shape[0]
    n = t // batch
    w, cc, sc = (jnp.asarray(a).astype(BF16) for a in _dft_tables(n))
    nr = n // tr
    return pl.pallas_call(
        functools.partial(_fourier_kernel, n),
        grid=(batch, nr),
        in_specs=[
            pl.BlockSpec((n, FNET_W), lambda b, r: (b, REST_UB_BLOCK)),
            pl.BlockSpec((tr, 2 * n), lambda b, r: (r, 0)),
            pl.BlockSpec((FNET_GROUP, FNET_GROUP), lambda b, r: (0, 0)),
            pl.BlockSpec((FNET_GROUP, FNET_GROUP), lambda b, r: (0, 0)),
        ],
        out_specs=pl.BlockSpec((tr, FNET_W), lambda b, r: (b * nr + r, 0)),
        out_shape=jax.ShapeDtypeStruct((t, FNET_W), BF16),
        scratch_shapes=[pltpu.VMEM((2 * n, FNET_W), BF16)],
        compiler_params=pltpu.CompilerParams(
            dimension_semantics=("arbitrary", "arbitrary"), vmem_limit_bytes=VMEM_LIMIT),
        name="fourier",
    )(rest, w, cc, sc)


def _merge_kernel(final, attn_ref, za_ref, fm_ref, zb_ref, ga_ref, gb_ref, x_ref, gate_ref,
                  wpa_ref, wpb_ref, wo_ref, *rest):
    ta = (attn_ref[...].astype(F32) * _silu(za_ref[...].astype(F32))).astype(BF16)
    ya = jnp.dot(ta, wpa_ref[0], preferred_element_type=F32)
    tb = (fm_ref[...].astype(F32) * _silu(zb_ref[...].astype(F32))).astype(BF16)
    yb = jnp.dot(tb, wpb_ref[0], preferred_element_type=F32)
    mix = (jax.nn.sigmoid(ga_ref[...].astype(F32)) * ya
           + jax.nn.sigmoid(gb_ref[...].astype(F32)) * yb).astype(BF16)
    y = jnp.dot(mix, wo_ref[0], preferred_element_type=F32)
    out = x_ref[...] + gate_ref[0] * y
    if final:
        fg_ref, o_ref = rest
        ms = jnp.mean(out * out, axis=-1, keepdims=True)
        o_ref[...] = out * lax.rsqrt(ms + EPS) * fg_ref[...]
    else:
        shift_ref, scale_ref, g_ref, o_ref, h_ref = rest
        o_ref[...] = out
        h_ref[...] = _modnorm(out, g_ref[0], scale_ref[0], shift_ref[0]).astype(BF16)


def _merge(rest, attn, fm, x2d, mods, layer, mod_row_of_tile, w_pa, w_pb, w_o, norm_g, final_g,
           tm):
    t = x2d.shape[0]
    final = layer == DEPTH - 1

    def rows(width, block=0):
        return pl.BlockSpec((tm, width), lambda i: (i, block))

    def weight(k):
        return pl.BlockSpec((1, k, D_MODEL), lambda i: (layer, 0, 0), pipeline_mode=pl.Buffered(1))

    in_specs = [
        rows(ATT_W), rows(ATT_W, REST_ZA_BLOCK), rows(FNET_W), rows(FNET_W, REST_ZB_BLOCK),
        rows(D_MODEL, REST_GA_BLOCK), rows(D_MODEL, REST_GB_BLOCK),
        rows(D_MODEL),
    ] + _mod_specs(layer, mod_row_of_tile, (MOD_GATE,)) + [
        weight(ATT_W), weight(FNET_W), weight(D_MODEL)]
    args = [attn, rest, fm, rest, rest, rest, x2d, mods, w_pa, w_pb, w_o]
    stream = jax.ShapeDtypeStruct((t, D_MODEL), F32)
    if final:
        in_specs += [pl.BlockSpec((1, D_MODEL), lambda i: (0, 0))]
        args += [final_g]
        out_specs, out_shape = rows(D_MODEL), stream
    else:
        in_specs += _mod_specs(layer + 1, mod_row_of_tile, (MOD_SHIFT, MOD_SCALE))
        in_specs += [pl.BlockSpec((1, 1, D_MODEL), lambda i: (layer + 1, 0, 0))]
        args += [mods, mods, norm_g]
        out_specs = [rows(D_MODEL), rows(D_MODEL)]
        out_shape = [stream, jax.ShapeDtypeStruct((t, D_MODEL), BF16)]
    return pl.pallas_call(
        functools.partial(_merge_kernel, final),
        grid=(t // tm,),
        in_specs=in_specs,
        out_specs=out_specs,
        out_shape=out_shape,
        compiler_params=pltpu.CompilerParams(
            dimension_semantics=("arbitrary",), vmem_limit_bytes=VMEM_LIMIT),
        name="merge",
    )(*args)


def _rope_tables(n_tokens):
    t = np.arange(n_tokens)
    pos = np.stack([t // GRID_W, t % GRID_W], axis=1).astype(np.float32)
    inv = (np.float32(ROPE_THETA) ** (-np.arange(ROPE_NFREQ, dtype=np.float32) / ROPE_NFREQ))
    ang = (pos[:, :, None] * inv[None, None, :]).astype(np.float32).astype(np.float64)
    cos, sin = np.cos(ang), np.sin(ang)
    zero = np.zeros_like(sin)
    c = np.stack([cos, cos], axis=2).reshape(n_tokens, HEAD_DIM)
    s1 = np.stack([-sin, zero], axis=2).reshape(n_tokens, HEAD_DIM)
    s2 = np.stack([zero, sin], axis=2).reshape(n_tokens, HEAD_DIM)
    return tuple(jnp.asarray(a.astype(np.float32)) for a in (c, s1, s2))


def _identity_tables(n_tokens):
    one = jnp.ones((n_tokens, HEAD_DIM), F32)
    zero = jnp.zeros((n_tokens, HEAD_DIM), F32)
    return one, zero, zero


def kernel(x, c, ctx, c_ctx, w_ada, b_ada, norm_g, w_in, q_norm_g, k_norm_g,
           w_proj_a, w_proj_b, w_out, final_g):
    batch, n_lat, d = x.shape
    n_ctx = ctx.shape[1]
    tm_norm, tm_merge, sub = 512, 256, 1024
    ctx_row = batch

    cvec = jnp.concatenate(
        [c, c_ctx[None], jnp.zeros((MOD_ROWS - batch - 1, d), F32)], axis=0)
    mods = _ada(cvec, w_ada, b_ada).reshape(DEPTH * MOD_ROWS * 3, 1, d)

    rope = _rope_tables(n_lat)
    no_rope = _identity_tables(batch * n_ctx)
    heads_per_tile = COL_TILE // HEAD_DIM
    qg = jnp.tile(q_norm_g * Q_SCALE, (1, heads_per_tile)).reshape(DEPTH, 1, COL_TILE)
    kg = jnp.tile(k_norm_g, (1, heads_per_tile)).reshape(DEPTH, 1, COL_TILE)
    merge_w = (w_proj_a, w_proj_b, w_out)
    norm_g3 = norm_g.reshape(DEPTH, 1, d)
    fg = final_g.reshape(1, d)

    def lat_row(tm):
        return lambda i: i // (n_lat // tm)

    def ctx_mod_row(i):
        return ctx_row

    xs = x.reshape(batch * n_lat, d)
    cs = ctx.reshape(batch * n_ctx, d)
    h = _modnorm_call(xs, mods, 0, lat_row(tm_norm), norm_g3, tm_norm)
    hc = _modnorm_call(cs, mods, 0, ctx_mod_row, norm_g3, tm_norm)
    for l in range(DEPTH):
        last = l == DEPTH - 1
        lo, hi = (K_TILE, V_TILE + 1) if last else (0, N_COL_TILES)
        pc = _in_proj(hc, w_in, l, qg, kg, no_rope, lo, hi, batch * n_ctx, sub)
        p = _in_proj(h, w_in, l, qg, kg, rope, 0, N_COL_TILES, n_lat, sub)

        kv_sets = [(p["k"], p["vt"]), (pc["k"], pc["vt"])]
        if l == 0:
            attn, *cast = _attention(p["q"], kv_sets, batch, 256, 512,
                                     casts=tuple(w.reshape(-1, d) for w in merge_w))
            w_pa, w_pb, w_o = (c.reshape(w.shape) for c, w in zip(cast, merge_w))
        else:
            attn = _attention(p["q"], kv_sets, batch, 256, 512)
        fm = _fourier(p["rest"], batch, 512)
        merged = _merge(p["rest"], attn, fm, xs, mods, l, lat_row(tm_merge), w_pa, w_pb, w_o,
                        norm_g3, fg, tm_merge)
        if last:
            xs = merged
        else:
            xs, h = merged
            attn_c = _attention(pc["q"], [(pc["k"], pc["vt"])], batch, n_ctx, n_ctx)
            fm_c = _fourier(pc["rest"], batch, n_ctx)
            cs, hc = _merge(pc["rest"], attn_c, fm_c, cs, mods, l, ctx_mod_row, w_pa, w_pb, w_o,
                            norm_g3, fg, tm_merge)
    return xs.reshape(batch, n_lat, d)
```

```python
import functools
import math

import numpy as np
import jax
import jax.numpy as jnp
from jax import lax
from jax.experimental import pallas as pl
from jax.experimental.pallas import tpu as pltpu

F32 = jnp.float32
BF16 = jnp.bfloat16

D_MODEL = 2048
DEPTH = 2
CTX_LEN = 256
GRID_W = 64
HEAD_DIM = 128
ATT_W = (3 * D_MODEL) // 4
N_Q_HEADS = ATT_W // HEAD_DIM
N_KV_HEADS = 4
Q_PER_KV = N_Q_HEADS // N_KV_HEADS
KV_W = N_KV_HEADS * HEAD_DIM
FNET_W = D_MODEL // 4
N_FNET_GROUPS = 4
FNET_GROUP = FNET_W // N_FNET_GROUPS
ROPE_THETA = 10000.0
ROPE_NFREQ = HEAD_DIM // 4
EPS = 1e-6
IN_W = 2 * ATT_W + 2 * KV_W + 2 * FNET_W + 2 * D_MODEL
Q_SCALE = HEAD_DIM ** -0.5 * math.log2(math.e)

V7X_VMEM_BYTES = 64 * 1024 * 1024
VMEM_LIMIT = V7X_VMEM_BYTES - 8 * 1024 * 1024

COL_TILE = 512
N_COL_TILES = IN_W // COL_TILE
K_TILE = ATT_W // COL_TILE
V_TILE = K_TILE + 1
ZB_TILE = (2 * ATT_W + 2 * KV_W + FNET_W) // COL_TILE
REST_W = IN_W - ATT_W - 2 * KV_W
REST_ZA_BLOCK = 0
REST_UB_BLOCK = ATT_W // FNET_W
REST_GA_BLOCK = 1
REST_GB_BLOCK = 2
REST_ZB_BLOCK = REST_W // FNET_W - 1
MOD_ROWS = 8


def _silu(x):
    return x * jax.nn.sigmoid(x)


def _ada_kernel(cv_ref, w_ref, b_ref, o_ref):
    s = _silu(cv_ref[...]).astype(BF16)
    o_ref[0] = jnp.dot(s, w_ref[0].astype(BF16), preferred_element_type=F32) + b_ref[0]


def _ada(cvec, w_ada, b_ada):
    tn = 1024
    n = 3 * D_MODEL
    return pl.pallas_call(
        _ada_kernel,
        grid=(DEPTH, n // tn),
        in_specs=[
            pl.BlockSpec((MOD_ROWS, D_MODEL), lambda l, j: (0, 0)),
            pl.BlockSpec((1, D_MODEL, tn), lambda l, j: (l, 0, j)),
            pl.BlockSpec((1, 1, tn), lambda l, j: (l, 0, j)),
        ],
        out_specs=pl.BlockSpec((1, MOD_ROWS, tn), lambda l, j: (l, 0, j)),
        out_shape=jax.ShapeDtypeStruct((DEPTH, MOD_ROWS, n), F32),
        compiler_params=pltpu.CompilerParams(
            dimension_semantics=("arbitrary", "arbitrary"), vmem_limit_bytes=VMEM_LIMIT),
        name="ada",
    )(cvec, w_ada, b_ada.reshape(DEPTH, 1, n))


def _weight_tile(step):
    return jnp.where(step < ZB_TILE, step, jnp.where(step < N_COL_TILES - 1, step + 1, ZB_TILE))


def _head_norm_rope(acc, head_mean, g, c, s1, s2):
    ms = jnp.dot((acc * acc).astype(BF16), head_mean, preferred_element_type=F32)
    y = acc * lax.rsqrt(ms + EPS) * g
    outs = []
    for hh in range(COL_TILE // HEAD_DIM):
        yh = y[:, hh * HEAD_DIM:(hh + 1) * HEAD_DIM]
        outs.append(yh * c + pltpu.roll(yh, HEAD_DIM - ROPE_NFREQ, 1) * s1
                    + pltpu.roll(yh, ROPE_NFREQ, 1) * s2)
    return jnp.concatenate(outs, axis=-1)


def _modnorm(xf, g, scale, shift):
    ms = jnp.mean(xf * xf, axis=-1, keepdims=True)
    return xf * lax.rsqrt(ms + EPS) * g * (1.0 + scale) + shift


def _modnorm_kernel(x_ref, shift_ref, scale_ref, g_ref, h_ref):
    h_ref[...] = _modnorm(x_ref[...], g_ref[0], scale_ref[0], shift_ref[0]).astype(BF16)


def _mod_specs(layer, mod_row_of_tile, parts):
    return [pl.BlockSpec((1, 1, D_MODEL),
                         lambda i, p=p: ((layer * MOD_ROWS + mod_row_of_tile(i)) * 3 + p, 0, 0))
            for p in parts]


MOD_SHIFT, MOD_SCALE, MOD_GATE = 0, 1, 2


def _modnorm_call(x2d, mods, layer, mod_row_of_tile, norm_g, tm):
    t = x2d.shape[0]
    return pl.pallas_call(
        _modnorm_kernel,
        grid=(t // tm,),
        in_specs=[pl.BlockSpec((tm, D_MODEL), lambda i: (i, 0))]
        + _mod_specs(layer, mod_row_of_tile, (MOD_SHIFT, MOD_SCALE))
        + [pl.BlockSpec((1, 1, D_MODEL), lambda i: (layer, 0, 0))],
        out_specs=pl.BlockSpec((tm, D_MODEL), lambda i: (i, 0)),
        out_shape=jax.ShapeDtypeStruct((t, D_MODEL), BF16),
        compiler_params=pltpu.CompilerParams(
            dimension_semantics=("arbitrary",), vmem_limit_bytes=VMEM_LIMIT),
        name="modnorm",
    )(x2d, mods, mods, norm_g)


def _in_proj_kernel(step_lo, names, sub, h_ref, w_ref, qg_ref, kg_ref, hm_ref,
                    c_ref, s1_ref, s2_ref, *out_refs):
    out = dict(zip(names, out_refs))
    step = pl.program_id(1) + step_lo
    row_tiles = [slice(r, r + sub) for r in range(0, h_ref.shape[0], sub)]

    def tiles():
        w = w_ref[0].astype(BF16)
        for rows in row_tiles:
            yield rows, jnp.dot(h_ref[rows, :], w, preferred_element_type=F32)

    def head_tiles(o_ref, g_ref):
        for rows, acc in tiles():
            o_ref[rows, :] = _head_norm_rope(
                acc, hm_ref[...], g_ref[0], c_ref[rows, :], s1_ref[rows, :],
                s2_ref[rows, :]).astype(BF16)

    if "q" in out:
        @pl.when(step < K_TILE)
        def _():
            head_tiles(out["q"], qg_ref)

    if "k" in out:
        @pl.when(step == K_TILE)
        def _():
            head_tiles(out["k"], kg_ref)

    if "vt" in out:
        @pl.when(step == V_TILE)
        def _():
            for rows, acc in tiles():
                out["vt"][:, rows] = acc.T.astype(BF16)

    if "rest" in out:
        @pl.when(step > V_TILE)
        def _():
            for rows, acc in tiles():
                out["rest"][rows, :] = acc.astype(BF16)


def _in_proj(h, w_in, layer, q_g, k_g, tables, step_lo, step_hi, group, sub):
    t = h.shape[0]
    c_tab, s1_tab, s2_tab = tables
    n_rest = REST_W // COL_TILE
    heads_per_tile = COL_TILE // HEAD_DIM
    head_mean = jnp.asarray(
        np.kron(np.eye(heads_per_tile), np.full((HEAD_DIM, HEAD_DIM), 1.0 / HEAD_DIM)), BF16)
    specs = {
        "q": (pl.BlockSpec((group, COL_TILE),
                           lambda i, j: (i, jnp.clip(j + step_lo, 0, K_TILE - 1))),
              jax.ShapeDtypeStruct((t, ATT_W), BF16), step_lo < K_TILE),
        "k": (pl.BlockSpec((group, COL_TILE), lambda i, j: (i, 0)),
              jax.ShapeDtypeStruct((t, KV_W), BF16), step_lo <= K_TILE < step_hi),
        "vt": (pl.BlockSpec((COL_TILE, group), lambda i, j: (0, i)),
               jax.ShapeDtypeStruct((KV_W, t), BF16), step_lo <= V_TILE < step_hi),
        "rest": (pl.BlockSpec((group, COL_TILE),
                              lambda i, j: (i, jnp.clip(j + step_lo - V_TILE - 1, 0, n_rest - 1))),
                 jax.ShapeDtypeStruct((t, REST_W), BF16), step_hi > V_TILE + 1),
    }
    names = tuple(n for n, (_, _, present) in specs.items() if present)
    tab_spec = pl.BlockSpec((group, HEAD_DIM), lambda i, j: (0, 0))
    gain_spec = pl.BlockSpec((1, 1, COL_TILE), lambda i, j: (layer, 0, 0))
    outs = pl.pallas_call(
        functools.partial(_in_proj_kernel, step_lo, names, sub),
        grid=(t // group, step_hi - step_lo),
        in_specs=[
            pl.BlockSpec((group, D_MODEL), lambda i, j: (i, 0)),
            pl.BlockSpec((1, D_MODEL, COL_TILE),
                         lambda i, j: (layer, 0, _weight_tile(j + step_lo))),
            gain_spec, gain_spec,
            pl.BlockSpec((COL_TILE, COL_TILE), lambda i, j: (0, 0)),
            tab_spec, tab_spec, tab_spec,
        ],
        out_specs=[specs[n][0] for n in names],
        out_shape=[specs[n][1] for n in names],
        compiler_params=pltpu.CompilerParams(
            dimension_semantics=("arbitrary", "arbitrary"), vmem_limit_bytes=VMEM_LIMIT),
        name="in_proj",
    )(h, w_in, q_g, k_g, head_mean, c_tab, s1_tab, s2_tab)
    return dict(zip(names, outs))


ONES_ROWS = 16


def _col_reduce8(x, op):
    rows, w = x.shape
    return op(x.reshape(rows // 8, 8, w), axis=0)


def _attn_kernel(chunks, tq, n_casts, q_ref, *rest):
    n_sets = 1 + max(c[0] for c in chunks)
    k_refs, vt_refs = rest[:n_sets], rest[n_sets:2 * n_sets]
    cast_in = rest[2 * n_sets:2 * n_sets + n_casts]
    o_ref = rest[2 * n_sets + n_casts]
    cast_out = rest[2 * n_sets + n_casts + 1:2 * n_sets + 2 * n_casts + 1]
    s_scr, m_scr = rest[2 * n_sets + 2 * n_casts + 1:]
    width = Q_PER_KV * tq
    for src, dst in zip(cast_in, cast_out):
        dst[...] = src[...].astype(BF16)

    @pl.when(pl.program_id(0) == 0)
    def _():
        s_scr[...] = jnp.zeros_like(s_scr)
        m_scr[...] = jnp.zeros_like(m_scr)

    q3 = q_ref[...]
    qs = jnp.concatenate(
        [q3[:, i * HEAD_DIM:(i + 1) * HEAD_DIM] for i in range(Q_PER_KV)], axis=0)
    q_t = qs.astype(F32).T.astype(BF16)
    m_prev = m_scr[...]
    m_run = jnp.full((8, width), -jnp.inf, F32)
    acc = jnp.zeros((HEAD_DIM + ONES_ROWS, width), F32)
    for t, r0, rows, s0 in chunks:
        p = jnp.exp2(s_scr[s0:s0 + rows, :] - m_prev).astype(BF16)
        vt_ones = jnp.concatenate(
            [vt_refs[t][:, r0:r0 + rows], jnp.ones((ONES_ROWS, rows), BF16)], axis=0)
        acc = acc + jnp.dot(vt_ones, p, preferred_element_type=F32)
        s_new = jnp.dot(k_refs[t][r0:r0 + rows, :], q_t, preferred_element_type=F32)
        s_scr[s0:s0 + rows, :] = s_new
        m_run = jnp.maximum(m_run, _col_reduce8(s_new, jnp.max))
    m_scr[...] = m_run.max(axis=0, keepdims=True)
    o_t = acc[:HEAD_DIM] / acc[HEAD_DIM:HEAD_DIM + 1]
    for i in range(Q_PER_KV):
        o_ref[:, i * HEAD_DIM:(i + 1) * HEAD_DIM] = o_t[:, i * tq:(i + 1) * tq].T.astype(BF16)


CAST_STEPS = 64


def _attention(q, kv_sets, batch, tq, key_chunk, casts=()):
    t = q.shape[0]
    nq = t // batch // tq
    n_tiles = batch * N_KV_HEADS * nq
    gw = Q_PER_KV * HEAD_DIM
    assert not casts or n_tiles + 1 >= CAST_STEPS
    cast_specs = [pl.BlockSpec((a.shape[0] // CAST_STEPS, a.shape[1]),
                               lambda n: (jnp.minimum(n, CAST_STEPS - 1), 0)) for a in casts]
    cast_shapes = [jax.ShapeDtypeStruct(a.shape, BF16) for a in casts]

    def tile(n):
        return n // (N_KV_HEADS * nq), (n // nq) % N_KV_HEADS, n % nq

    def score_tile(n):
        return tile(jnp.minimum(n, n_tiles - 1))

    def out_tile(n):
        return tile(jnp.maximum(n - 1, 0))

    def q_map(n):
        b, g, i = score_tile(n)
        return b * nq + i, g

    def o_map(n):
        b, g, i = out_tile(n)
        return b * nq + i, g

    def k_map(n):
        b, g, _ = score_tile(n)
        return b, g

    def vt_map(n):
        b, g, _ = out_tile(n)
        return g, b

    chunks, k_specs, vt_specs, s_rows = [], [], [], 0
    for s, (k, _) in enumerate(kv_sets):
        m_t = k.shape[0] // batch
        step = min(key_chunk, m_t)
        for r0 in range(0, m_t, step):
            chunks.append((s, r0, step, s_rows + r0))
        s_rows += m_t
        k_specs.append(pl.BlockSpec((m_t, HEAD_DIM), k_map))
        vt_specs.append(pl.BlockSpec((HEAD_DIM, m_t), vt_map))
    outs = pl.pallas_call(
        functools.partial(_attn_kernel, tuple(chunks), tq, len(casts)),
        grid=(n_tiles + 1,),
        in_specs=[pl.BlockSpec((tq, gw), q_map)] + k_specs + vt_specs + cast_specs,
        out_specs=[pl.BlockSpec((tq, gw), o_map)] + cast_specs,
        out_shape=[jax.ShapeDtypeStruct((t, ATT_W), BF16)] + cast_shapes,
        scratch_shapes=[pltpu.VMEM((s_rows, Q_PER_KV * tq), F32),
                        pltpu.VMEM((1, Q_PER_KV * tq), F32)],
        compiler_params=pltpu.CompilerParams(
            dimension_semantics=("arbitrary",), vmem_limit_bytes=VMEM_LIMIT),
        name="attn",
    )(q, *[k for k, _ in kv_sets], *[vt for _, vt in kv_sets], *casts)
    return outs if casts else outs[0]


DFT_PAD_ROWS = 16


def _dft_tables(n):
    h = n // 2
    pos = np.arange(n, dtype=np.int64)
    ang = 2.0 * np.pi * ((pos[:h + 1, None] * pos[None, :]) % n).astype(np.float64) / n
    c_top = np.zeros((h + DFT_PAD_ROWS, n))
    c_top[:h + 1] = np.cos(ang) / np.sqrt(n)
    s_top = np.sin(ang[:h]) / np.sqrt(n)
    flip = np.zeros((h, h))
    flip[np.arange(h), (-np.arange(h)) % h] = 1.0
    ch = np.arange(FNET_GROUP, dtype=np.int64)
    ang_c = 2.0 * np.pi * ((ch[:, None] * ch[None, :]) % FNET_GROUP) / FNET_GROUP
    cc, sc = np.cos(ang_c) / np.sqrt(FNET_GROUP), np.sin(ang_c) / np.sqrt(FNET_GROUP)
    return tuple(a.astype(np.float32) for a in (c_top, s_top, flip, cc, sc))


def _fourier_kernel(n, u_ref, ct_ref, st_ref, flip_ref, cc_ref, sc_ref, o_ref, a_scr, b_scr):
    h = n // 2
    for g in range(N_FNET_GROUPS):
        cols = slice(g * FNET_GROUP, (g + 1) * FNET_GROUP)
        ug = u_ref[:, cols]
        a_scr[:, cols] = jnp.dot(ug, cc_ref[...], preferred_element_type=F32).astype(BF16)
        b_scr[:, cols] = jnp.dot(ug, sc_ref[...], preferred_element_type=F32).astype(BF16)
    p = jnp.dot(ct_ref[...], a_scr[...], preferred_element_type=F32)
    q = jnp.dot(st_ref[...], b_scr[...], preferred_element_type=F32)
    o_ref[0:h, :] = (p[:h] - q).astype(BF16)
    is_row0 = lax.broadcasted_iota(jnp.int32, (h, FNET_W), 0) == 0
    z = jnp.where(is_row0, p[h:h + 1], p[:h] + q).astype(BF16)
    o_ref[h:n, :] = jnp.dot(flip_ref[...], z, preferred_element_type=F32).astype(BF16)


def _fourier(rest, batch):
    t = rest.shape[0]
    n = t // batch
    tables = [jnp.asarray(a).astype(BF16) for a in _dft_tables(n)]

    def whole(a):
        return pl.BlockSpec(a.shape, lambda b: (0, 0), pipeline_mode=pl.Buffered(1))

    return pl.pallas_call(
        functools.partial(_fourier_kernel, n),
        grid=(batch,),
        in_specs=[pl.BlockSpec((n, FNET_W), lambda b: (b, REST_UB_BLOCK))]
        + [whole(a) for a in tables],
        out_specs=pl.BlockSpec((n, FNET_W), lambda b: (b, 0)),
        out_shape=jax.ShapeDtypeStruct((t, FNET_W), BF16),
        scratch_shapes=[pltpu.VMEM((n, FNET_W), BF16), pltpu.VMEM((n, FNET_W), BF16)],
        compiler_params=pltpu.CompilerParams(
            dimension_semantics=("arbitrary",), vmem_limit_bytes=VMEM_LIMIT),
        name="fourier",
    )(rest, *tables)


def _merge_kernel(final, attn_ref, za_ref, fm_ref, zb_ref, ga_ref, gb_ref, x_ref, gate_ref,
                  wpa_ref, wpb_ref, wo_ref, *rest):
    ta = (attn_ref[...].astype(F32) * _silu(za_ref[...].astype(F32))).astype(BF16)
    ya = jnp.dot(ta, wpa_ref[0], preferred_element_type=F32)
    tb = (fm_ref[...].astype(F32) * _silu(zb_ref[...].astype(F32))).astype(BF16)
    yb = jnp.dot(tb, wpb_ref[0], preferred_element_type=F32)
    mix = (jax.nn.sigmoid(ga_ref[...].astype(F32)) * ya
           + jax.nn.sigmoid(gb_ref[...].astype(F32)) * yb).astype(BF16)
    y = jnp.dot(mix, wo_ref[0], preferred_element_type=F32)
    out = x_ref[...] + gate_ref[0] * y
    if final:
        fg_ref, o_ref = rest
        ms = jnp.mean(out * out, axis=-1, keepdims=True)
        o_ref[...] = out * lax.rsqrt(ms + EPS) * fg_ref[...]
    else:
        shift_ref, scale_ref, g_ref, o_ref, h_ref = rest
        o_ref[...] = out
        h_ref[...] = _modnorm(out, g_ref[0], scale_ref[0], shift_ref[0]).astype(BF16)


def _merge(rest, attn, fm, x2d, mods, layer, mod_row_of_tile, w_pa, w_pb, w_o, norm_g, final_g,
           tm):
    t = x2d.shape[0]
    final = layer == DEPTH - 1

    def rows(width, block=0):
        return pl.BlockSpec((tm, width), lambda i: (i, block))

    def weight(k):
        return pl.BlockSpec((1, k, D_MODEL), lambda i: (layer, 0, 0), pipeline_mode=pl.Buffered(1))

    in_specs = [
        rows(ATT_W), rows(ATT_W, REST_ZA_BLOCK), rows(FNET_W), rows(FNET_W, REST_ZB_BLOCK),
        rows(D_MODEL, REST_GA_BLOCK), rows(D_MODEL, REST_GB_BLOCK),
        rows(D_MODEL),
    ] + _mod_specs(layer, mod_row_of_tile, (MOD_GATE,)) + [
        weight(ATT_W), weight(FNET_W), weight(D_MODEL)]
    args = [attn, rest, fm, rest, rest, rest, x2d, mods, w_pa, w_pb, w_o]
    stream = jax.ShapeDtypeStruct((t, D_MODEL), F32)
    if final:
        in_specs += [pl.BlockSpec((1, D_MODEL), lambda i: (0, 0))]
        args += [final_g]
        out_specs, out_shape = rows(D_MODEL), stream
    else:
        in_specs += _mod_specs(layer + 1, mod_row_of_tile, (MOD_SHIFT, MOD_SCALE))
        in_specs += [pl.BlockSpec((1, 1, D_MODEL), lambda i: (layer + 1, 0, 0))]
        args += [mods, mods, norm_g]
        out_specs = [rows(D_MODEL), rows(D_MODEL)]
        out_shape = [stream, jax.ShapeDtypeStruct((t, D_MODEL), BF16)]
    return pl.pallas_call(
        functools.partial(_merge_kernel, final),
        grid=(t // tm,),
        in_specs=in_specs,
        out_specs=out_specs,
        out_shape=out_shape,
        compiler_params=pltpu.CompilerParams(
            dimension_semantics=("arbitrary",), vmem_limit_bytes=VMEM_LIMIT),
        name="merge",
    )(*args)


def _rope_tables(n_tokens):
    t = np.arange(n_tokens)
    pos = np.stack([t // GRID_W, t % GRID_W], axis=1).astype(np.float32)
    inv = (np.float32(ROPE_THETA) ** (-np.arange(ROPE_NFREQ, dtype=np.float32) / ROPE_NFREQ))
    ang = (pos[:, :, None] * inv[None, None, :]).astype(np.float32).astype(np.float64)
    cos, sin = np.cos(ang), np.sin(ang)
    zero = np.zeros_like(sin)
    c = np.stack([cos, cos], axis=2).reshape(n_tokens, HEAD_DIM)
    s1 = np.stack([-sin, zero], axis=2).reshape(n_tokens, HEAD_DIM)
    s2 = np.stack([zero, sin], axis=2).reshape(n_tokens, HEAD_DIM)
    return tuple(jnp.asarray(a.astype(np.float32)) for a in (c, s1, s2))


def _identity_tables(n_tokens):
    one = jnp.ones((n_tokens, HEAD_DIM), F32)
    zero = jnp.zeros((n_tokens, HEAD_DIM), F32)
    return one, zero, zero


def kernel(x, c, ctx, c_ctx, w_ada, b_ada, norm_g, w_in, q_norm_g, k_norm_g,
           w_proj_a, w_proj_b, w_out, final_g):
    batch, n_lat, d = x.shape
    n_ctx = ctx.shape[1]
    tm_norm, tm_merge, sub = 512, 256, 1024
    ctx_row = batch

    cvec = jnp.concatenate(
        [c, c_ctx[None], jnp.zeros((MOD_ROWS - batch - 1, d), F32)], axis=0)
    mods = _ada(cvec, w_ada, b_ada).reshape(DEPTH * MOD_ROWS * 3, 1, d)

    rope = _rope_tables(n_lat)
    no_rope = _identity_tables(batch * n_ctx)
    heads_per_tile = COL_TILE // HEAD_DIM
    qg = jnp.tile(q_norm_g * Q_SCALE, (1, heads_per_tile)).reshape(DEPTH, 1, COL_TILE)
    kg = jnp.tile(k_norm_g, (1, heads_per_tile)).reshape(DEPTH, 1, COL_TILE)
    merge_w = (w_proj_a, w_proj_b, w_out)
    norm_g3 = norm_g.reshape(DEPTH, 1, d)
    fg = final_g.reshape(1, d)

    def lat_row(tm):
        return lambda i: i // (n_lat // tm)

    def ctx_mod_row(i):
        return ctx_row

    xs = x.reshape(batch * n_lat, d)
    cs = ctx.reshape(batch * n_ctx, d)
    h = _modnorm_call(xs, mods, 0, lat_row(tm_norm), norm_g3, tm_norm)
    hc = _modnorm_call(cs, mods, 0, ctx_mod_row, norm_g3, tm_norm)
    for l in range(DEPTH):
        last = l == DEPTH - 1
        lo, hi = (K_TILE, V_TILE + 1) if last else (0, N_COL_TILES)
        pc = _in_proj(hc, w_in, l, qg, kg, no_rope, lo, hi, batch * n_ctx, sub)
        p = _in_proj(h, w_in, l, qg, kg, rope, 0, N_COL_TILES, n_lat, sub)

        kv_sets = [(p["k"], p["vt"]), (pc["k"], pc["vt"])]
        if l == 0:
            attn, *cast = _attention(p["q"], kv_sets, batch, 256, 512,
                                     casts=tuple(w.reshape(-1, d) for w in merge_w))
            w_pa, w_pb, w_o = (c.reshape(w.shape) for c, w in zip(cast, merge_w))
        else:
            attn = _attention(p["q"], kv_sets, batch, 256, 512)
        fm = _fourier(p["rest"], batch)
        merged = _merge(p["rest"], attn, fm, xs, mods, l, lat_row(tm_merge), w_pa, w_pb, w_o,
                        norm_g3, fg, tm_merge)
        if last:
            xs = merged
        else:
            xs, h = merged
            attn_c = _attention(pc["q"], [(pc["k"], pc["vt"])], batch, n_ctx, n_ctx)
            fm_c = _fourier(pc["rest"], batch)
            cs, hc = _merge(pc["rest"], attn_c, fm_c, cs, mods, l, ctx_mod_row, w_pa, w_pb, w_o,
                            norm_g3, fg, tm_merge)
    return xs.reshape(batch, n_lat, d)
```

```python
import functools
import math

import numpy as np
import jax
import jax.numpy as jnp
from jax import lax
from jax.experimental import pallas as pl
from jax.experimental.pallas import tpu as pltpu

F32 = jnp.float32
BF16 = jnp.bfloat16

D_MODEL = 2048
DEPTH = 2
CTX_LEN = 256
GRID_W = 64
HEAD_DIM = 128
ATT_W = (3 * D_MODEL) // 4
N_Q_HEADS = ATT_W // HEAD_DIM
N_KV_HEADS = 4
Q_PER_KV = N_Q_HEADS // N_KV_HEADS
KV_W = N_KV_HEADS * HEAD_DIM
FNET_W = D_MODEL // 4
N_FNET_GROUPS = 4
FNET_GROUP = FNET_W // N_FNET_GROUPS
ROPE_THETA = 10000.0
ROPE_NFREQ = HEAD_DIM // 4
EPS = 1e-6
IN_W = 2 * ATT_W + 2 * KV_W + 2 * FNET_W + 2 * D_MODEL
Q_SCALE = HEAD_DIM ** -0.5 * math.log2(math.e)

V7X_VMEM_BYTES = 64 * 1024 * 1024
VMEM_LIMIT = V7X_VMEM_BYTES - 8 * 1024 * 1024

COL_TILE = 512
N_COL_TILES = IN_W // COL_TILE
K_TILE = ATT_W // COL_TILE
V_TILE = K_TILE + 1
ZB_TILE = (2 * ATT_W + 2 * KV_W + FNET_W) // COL_TILE
REST_W = IN_W - ATT_W - 2 * KV_W
REST_ZA_BLOCK = 0
REST_UB_BLOCK = ATT_W // FNET_W
REST_GA_BLOCK = 1
REST_GB_BLOCK = 2
REST_ZB_BLOCK = REST_W // FNET_W - 1
MOD_ROWS = 8


def _silu(x):
    return x * jax.nn.sigmoid(x)


def _ada_kernel(cv_ref, w_ref, b_ref, o_ref):
    s = _silu(cv_ref[...]).astype(BF16)
    o_ref[0] = jnp.dot(s, w_ref[0].astype(BF16), preferred_element_type=F32) + b_ref[0]


def _ada(cvec, w_ada, b_ada):
    tn = 1024
    n = 3 * D_MODEL
    return pl.pallas_call(
        _ada_kernel,
        grid=(DEPTH, n // tn),
        in_specs=[
            pl.BlockSpec((MOD_ROWS, D_MODEL), lambda l, j: (0, 0)),
            pl.BlockSpec((1, D_MODEL, tn), lambda l, j: (l, 0, j)),
            pl.BlockSpec((1, 1, tn), lambda l, j: (l, 0, j)),
        ],
        out_specs=pl.BlockSpec((1, MOD_ROWS, tn), lambda l, j: (l, 0, j)),
        out_shape=jax.ShapeDtypeStruct((DEPTH, MOD_ROWS, n), F32),
        compiler_params=pltpu.CompilerParams(
            dimension_semantics=("arbitrary", "arbitrary"), vmem_limit_bytes=VMEM_LIMIT),
        name="ada",
    )(cvec, w_ada, b_ada.reshape(DEPTH, 1, n))


def _weight_tile(step):
    return jnp.where(step < ZB_TILE, step, jnp.where(step < N_COL_TILES - 1, step + 1, ZB_TILE))


def _head_norm_rope(acc, head_mean, g, c, s1, s2):
    ms = jnp.dot((acc * acc).astype(BF16), head_mean, preferred_element_type=F32)
    y = acc * lax.rsqrt(ms + EPS) * g
    outs = []
    for hh in range(COL_TILE // HEAD_DIM):
        yh = y[:, hh * HEAD_DIM:(hh + 1) * HEAD_DIM]
        outs.append(yh * c + pltpu.roll(yh, HEAD_DIM - ROPE_NFREQ, 1) * s1
                    + pltpu.roll(yh, ROPE_NFREQ, 1) * s2)
    return jnp.concatenate(outs, axis=-1)


def _modnorm(xf, g, scale, shift):
    ms = jnp.mean(xf * xf, axis=-1, keepdims=True)
    return xf * lax.rsqrt(ms + EPS) * g * (1.0 + scale) + shift


def _modnorm_kernel(x_ref, shift_ref, scale_ref, g_ref, h_ref):
    h_ref[...] = _modnorm(x_ref[...], g_ref[0], scale_ref[0], shift_ref[0]).astype(BF16)


def _mod_specs(layer, mod_row_of_tile, parts):
    return [pl.BlockSpec((1, 1, D_MODEL),
                         lambda i, p=p: ((layer * MOD_ROWS + mod_row_of_tile(i)) * 3 + p, 0, 0))
            for p in parts]


MOD_SHIFT, MOD_SCALE, MOD_GATE = 0, 1, 2


def _modnorm_call(x2d, mods, layer, mod_row_of_tile, norm_g, tm):
    t = x2d.shape[0]
    return pl.pallas_call(
        _modnorm_kernel,
        grid=(t // tm,),
        in_specs=[pl.BlockSpec((tm, D_MODEL), lambda i: (i, 0))]
        + _mod_specs(layer, mod_row_of_tile, (MOD_SHIFT, MOD_SCALE))
        + [pl.BlockSpec((1, 1, D_MODEL), lambda i: (layer, 0, 0))],
        out_specs=pl.BlockSpec((tm, D_MODEL), lambda i: (i, 0)),
        out_shape=jax.ShapeDtypeStruct((t, D_MODEL), BF16),
        compiler_params=pltpu.CompilerParams(
            dimension_semantics=("arbitrary",), vmem_limit_bytes=VMEM_LIMIT),
        name="modnorm",
    )(x2d, mods, mods, norm_g)


def _in_proj_kernel(step_lo, names, sub, h_ref, w_ref, qg_ref, kg_ref, hm_ref,
                    c_ref, s1_ref, s2_ref, *out_refs):
    out = dict(zip(names, out_refs))
    step = pl.program_id(1) + step_lo
    row_tiles = [slice(r, r + sub) for r in range(0, h_ref.shape[0], sub)]

    def tiles():
        w = w_ref[0].astype(BF16)
        for rows in row_tiles:
            yield rows, jnp.dot(h_ref[rows, :], w, preferred_element_type=F32)

    def head_tiles(g_ref):
        for rows, acc in tiles():
            yield rows, _head_norm_rope(acc, hm_ref[...], g_ref[0], c_ref[rows, :],
                                        s1_ref[rows, :], s2_ref[rows, :])

    if "qt" in out:
        @pl.when(step < K_TILE)
        def _():
            for rows, qh in head_tiles(qg_ref):
                out["qt"][:, rows] = qh.T.astype(BF16)

    if "k" in out:
        @pl.when(step == K_TILE)
        def _():
            for rows, kh in head_tiles(kg_ref):
                out["k"][rows, :] = kh.astype(BF16)

    if "vt" in out:
        @pl.when(step == V_TILE)
        def _():
            for rows, acc in tiles():
                out["vt"][:, rows] = acc.T.astype(BF16)

    if "rest" in out:
        @pl.when(step > V_TILE)
        def _():
            for rows, acc in tiles():
                out["rest"][rows, :] = acc.astype(BF16)


def _in_proj(h, w_in, layer, q_g, k_g, tables, step_lo, step_hi, group, sub):
    t = h.shape[0]
    c_tab, s1_tab, s2_tab = tables
    n_rest = REST_W // COL_TILE
    heads_per_tile = COL_TILE // HEAD_DIM
    head_mean = jnp.asarray(
        np.kron(np.eye(heads_per_tile), np.full((HEAD_DIM, HEAD_DIM), 1.0 / HEAD_DIM)), BF16)
    specs = {
        "qt": (pl.BlockSpec((COL_TILE, group),
                            lambda i, j: (jnp.clip(j + step_lo, 0, K_TILE - 1), i)),
               jax.ShapeDtypeStruct((ATT_W, t), BF16), step_lo < K_TILE),
        "k": (pl.BlockSpec((group, COL_TILE), lambda i, j: (i, 0)),
              jax.ShapeDtypeStruct((t, KV_W), BF16), step_lo <= K_TILE < step_hi),
        "vt": (pl.BlockSpec((COL_TILE, group), lambda i, j: (0, i)),
               jax.ShapeDtypeStruct((KV_W, t), BF16), step_lo <= V_TILE < step_hi),
        "rest": (pl.BlockSpec((group, COL_TILE),
                              lambda i, j: (i, jnp.clip(j + step_lo - V_TILE - 1, 0, n_rest - 1))),
                 jax.ShapeDtypeStruct((t, REST_W), BF16), step_hi > V_TILE + 1),
    }
    names = tuple(n for n, (_, _, present) in specs.items() if present)
    tab_spec = pl.BlockSpec((group, HEAD_DIM), lambda i, j: (0, 0))
    gain_spec = pl.BlockSpec((1, 1, COL_TILE), lambda i, j: (layer, 0, 0))
    outs = pl.pallas_call(
        functools.partial(_in_proj_kernel, step_lo, names, sub),
        grid=(t // group, step_hi - step_lo),
        in_specs=[
            pl.BlockSpec((group, D_MODEL), lambda i, j: (i, 0)),
            pl.BlockSpec((1, D_MODEL, COL_TILE),
                         lambda i, j: (layer, 0, _weight_tile(j + step_lo))),
            gain_spec, gain_spec,
            pl.BlockSpec((COL_TILE, COL_TILE), lambda i, j: (0, 0)),
            tab_spec, tab_spec, tab_spec,
        ],
        out_specs=[specs[n][0] for n in names],
        out_shape=[specs[n][1] for n in names],
        compiler_params=pltpu.CompilerParams(
            dimension_semantics=("arbitrary", "arbitrary"), vmem_limit_bytes=VMEM_LIMIT),
        name="in_proj",
    )(h, w_in, q_g, k_g, head_mean, c_tab, s1_tab, s2_tab)
    return dict(zip(names, outs))


ONES_ROWS = 16


def _col_reduce8(x, op):
    rows, w = x.shape
    return op(x.reshape(rows // 8, 8, w), axis=0)


def _attn_kernel(chunks, tq, n_casts, q_ref, *rest):
    n_sets = 1 + max(c[0] for c in chunks)
    k_refs, vt_refs = rest[:n_sets], rest[n_sets:2 * n_sets]
    cast_in = rest[2 * n_sets:2 * n_sets + n_casts]
    o_ref = rest[2 * n_sets + n_casts]
    cast_out = rest[2 * n_sets + n_casts + 1:2 * n_sets + 2 * n_casts + 1]
    s_scr, m_scr = rest[2 * n_sets + 2 * n_casts + 1:]
    width = Q_PER_KV * tq
    for src, dst in zip(cast_in, cast_out):
        dst[...] = pltpu.bitcast(src[...].astype(BF16), jnp.uint32)

    @pl.when(pl.program_id(0) == 0)
    def _():
        s_scr[...] = jnp.zeros_like(s_scr)
        m_scr[...] = jnp.zeros_like(m_scr)

    q_t = jnp.concatenate(
        [q_ref[i * HEAD_DIM:(i + 1) * HEAD_DIM, :] for i in range(Q_PER_KV)], axis=1)
    m_prev = m_scr[...]
    m_run = jnp.full((8, width), -jnp.inf, F32)
    acc = jnp.zeros((HEAD_DIM + ONES_ROWS, width), F32)
    for t, r0, rows, s0 in chunks:
        p = jnp.exp2(s_scr[s0:s0 + rows, :] - m_prev).astype(BF16)
        vt_ones = jnp.concatenate(
            [vt_refs[t][:, r0:r0 + rows], jnp.ones((ONES_ROWS, rows), BF16)], axis=0)
        acc = acc + jnp.dot(vt_ones, p, preferred_element_type=F32)
        s_new = jnp.dot(k_refs[t][r0:r0 + rows, :], q_t, preferred_element_type=F32)
        s_scr[s0:s0 + rows, :] = s_new
        m_run = jnp.maximum(m_run, _col_reduce8(s_new, jnp.max))
    m_scr[...] = m_run.max(axis=0, keepdims=True)
    o_t = acc[:HEAD_DIM] / acc[HEAD_DIM:HEAD_DIM + 1]
    for i in range(Q_PER_KV):
        o_ref[:, i * HEAD_DIM:(i + 1) * HEAD_DIM] = o_t[:, i * tq:(i + 1) * tq].T.astype(BF16)


CAST_STEPS = 64


def _attention(qt, kv_sets, batch, tq, key_chunk, casts=()):
    t = qt.shape[1]
    nq = t // batch // tq
    n_tiles = batch * N_KV_HEADS * nq
    gw = Q_PER_KV * HEAD_DIM
    assert not casts or n_tiles + 1 >= CAST_STEPS

    def cast_spec(rows):
        return pl.BlockSpec((rows // CAST_STEPS, D_MODEL),
                            lambda n: (jnp.minimum(n, CAST_STEPS - 1), 0))

    cast_in_specs = [cast_spec(a.shape[0]) for a in casts]
    cast_out_specs = [cast_spec(a.shape[0] // 2) for a in casts]
    cast_shapes = [jax.ShapeDtypeStruct((a.shape[0] // 2, a.shape[1]), jnp.uint32)
                   for a in casts]

    def tile(n):
        return n // (N_KV_HEADS * nq), (n // nq) % N_KV_HEADS, n % nq

    def score_tile(n):
        return tile(jnp.minimum(n, n_tiles - 1))

    def out_tile(n):
        return tile(jnp.maximum(n - 1, 0))

    def q_map(n):
        b, g, i = score_tile(n)
        return g, b * nq + i

    def o_map(n):
        b, g, i = out_tile(n)
        return b * nq + i, g

    def k_map(n):
        b, g, _ = score_tile(n)
        return b, g

    def vt_map(n):
        b, g, _ = out_tile(n)
        return g, b

    chunks, k_specs, vt_specs, s_rows = [], [], [], 0
    for s, (k, _) in enumerate(kv_sets):
        m_t = k.shape[0] // batch
        step = min(key_chunk, m_t)
        for r0 in range(0, m_t, step):
            chunks.append((s, r0, step, s_rows + r0))
        s_rows += m_t
        k_specs.append(pl.BlockSpec((m_t, HEAD_DIM), k_map))
        vt_specs.append(pl.BlockSpec((HEAD_DIM, m_t), vt_map))
    outs = pl.pallas_call(
        functools.partial(_attn_kernel, tuple(chunks), tq, len(casts)),
        grid=(n_tiles + 1,),
        in_specs=[pl.BlockSpec((gw, tq), q_map)] + k_specs + vt_specs + cast_in_specs,
        out_specs=[pl.BlockSpec((tq, gw), o_map)] + cast_out_specs,
        out_shape=[jax.ShapeDtypeStruct((t, ATT_W), BF16)] + cast_shapes,
        scratch_shapes=[pltpu.VMEM((s_rows, Q_PER_KV * tq), F32),
                        pltpu.VMEM((1, Q_PER_KV * tq), F32)],
        compiler_params=pltpu.CompilerParams(
            dimension_semantics=("arbitrary",), vmem_limit_bytes=VMEM_LIMIT),
        name="attn",
    )(qt, *[k for k, _ in kv_sets], *[vt for _, vt in kv_sets], *casts)
    return outs if casts else outs[0]


DFT_PAD_ROWS = 16


def _dft_tables(n):
    h = n // 2
    pos = np.arange(n, dtype=np.int64)
    ang = 2.0 * np.pi * ((pos[:h + 1, None] * pos[None, :]) % n).astype(np.float64) / n
    c_top = np.zeros((h + DFT_PAD_ROWS, n))
    c_top[:h + 1] = np.cos(ang) / np.sqrt(n)
    s_top = np.sin(ang[:h]) / np.sqrt(n)
    flip = np.zeros((h, h))
    flip[np.arange(h), (-np.arange(h)) % h] = 1.0
    ch = np.arange(FNET_GROUP, dtype=np.int64)
    ang_c = 2.0 * np.pi * ((ch[:, None] * ch[None, :]) % FNET_GROUP) / FNET_GROUP
    cc, sc = np.cos(ang_c) / np.sqrt(FNET_GROUP), np.sin(ang_c) / np.sqrt(FNET_GROUP)
    return tuple(a.astype(np.float32) for a in (c_top, s_top, flip, cc, sc))


def _fourier_kernel(n, u_ref, ct_ref, st_ref, flip_ref, cc_ref, sc_ref, o_ref, a_scr, b_scr):
    h = n // 2
    for g in range(N_FNET_GROUPS):
        cols = slice(g * FNET_GROUP, (g + 1) * FNET_GROUP)
        ug = u_ref[:, cols]
        a_scr[:, cols] = jnp.dot(ug, cc_ref[...], preferred_element_type=F32).astype(BF16)
        b_scr[:, cols] = jnp.dot(ug, sc_ref[...], preferred_element_type=F32).astype(BF16)
    p = jnp.dot(ct_ref[...], a_scr[...], preferred_element_type=F32)
    q = jnp.dot(st_ref[...], b_scr[...], preferred_element_type=F32)
    o_ref[0:h, :] = (p[:h] - q).astype(BF16)
    is_row0 = lax.broadcasted_iota(jnp.int32, (h, FNET_W), 0) == 0
    z = jnp.where(is_row0, p[h:h + 1], p[:h] + q).astype(BF16)
    o_ref[h:n, :] = jnp.dot(flip_ref[...], z, preferred_element_type=F32).astype(BF16)


def _fourier(rest, batch):
    t = rest.shape[0]
    n = t // batch
    tables = [jnp.asarray(a).astype(BF16) for a in _dft_tables(n)]

    def whole(a):
        return pl.BlockSpec(a.shape, lambda b: (0, 0), pipeline_mode=pl.Buffered(1))

    return pl.pallas_call(
        functools.partial(_fourier_kernel, n),
        grid=(batch,),
        in_specs=[pl.BlockSpec((n, FNET_W), lambda b: (b, REST_UB_BLOCK))]
        + [whole(a) for a in tables],
        out_specs=pl.BlockSpec((n, FNET_W), lambda b: (b, 0)),
        out_shape=jax.ShapeDtypeStruct((t, FNET_W), BF16),
        scratch_shapes=[pltpu.VMEM((n, FNET_W), BF16), pltpu.VMEM((n, FNET_W), BF16)],
        compiler_params=pltpu.CompilerParams(
            dimension_semantics=("arbitrary",), vmem_limit_bytes=VMEM_LIMIT),
        name="fourier",
    )(rest, *tables)


def _merge_kernel(final, attn_ref, za_ref, fm_ref, zb_ref, ga_ref, gb_ref, x_ref, gate_ref,
                  wpa_ref, wpb_ref, wo_ref, *rest):
    def weight(w_ref):
        return pltpu.bitcast(w_ref[0], BF16)

    ta = (attn_ref[...].astype(F32) * _silu(za_ref[...].astype(F32))).astype(BF16)
    ya = jnp.dot(ta, weight(wpa_ref), preferred_element_type=F32)
    tb = (fm_ref[...].astype(F32) * _silu(zb_ref[...].astype(F32))).astype(BF16)
    yb = jnp.dot(tb, weight(wpb_ref), preferred_element_type=F32)
    mix = (jax.nn.sigmoid(ga_ref[...].astype(F32)) * ya
           + jax.nn.sigmoid(gb_ref[...].astype(F32)) * yb).astype(BF16)
    y = jnp.dot(mix, weight(wo_ref), preferred_element_type=F32)
    out = x_ref[...] + gate_ref[0] * y
    if final:
        fg_ref, o_ref = rest
        ms = jnp.mean(out * out, axis=-1, keepdims=True)
        o_ref[...] = out * lax.rsqrt(ms + EPS) * fg_ref[...]
    else:
        shift_ref, scale_ref, g_ref, o_ref, h_ref = rest
        o_ref[...] = out
        h_ref[...] = _modnorm(out, g_ref[0], scale_ref[0], shift_ref[0]).astype(BF16)


def _merge(rest, attn, fm, x2d, mods, layer, mod_row_of_tile, w_pa, w_pb, w_o, norm_g, final_g,
           tm):
    t = x2d.shape[0]
    final = layer == DEPTH - 1

    def rows(width, block=0):
        return pl.BlockSpec((tm, width), lambda i: (i, block))

    def weight(k):
        return pl.BlockSpec((1, k // 2, D_MODEL), lambda i: (layer, 0, 0),
                            pipeline_mode=pl.Buffered(1))

    in_specs = [
        rows(ATT_W), rows(ATT_W, REST_ZA_BLOCK), rows(FNET_W), rows(FNET_W, REST_ZB_BLOCK),
        rows(D_MODEL, REST_GA_BLOCK), rows(D_MODEL, REST_GB_BLOCK),
        rows(D_MODEL),
    ] + _mod_specs(layer, mod_row_of_tile, (MOD_GATE,)) + [
        weight(ATT_W), weight(FNET_W), weight(D_MODEL)]
    args = [attn, rest, fm, rest, rest, rest, x2d, mods, w_pa, w_pb, w_o]
    stream = jax.ShapeDtypeStruct((t, D_MODEL), F32)
    if final:
        in_specs += [pl.BlockSpec((1, D_MODEL), lambda i: (0, 0))]
        args += [final_g]
        out_specs, out_shape = rows(D_MODEL), stream
    else:
        in_specs += _mod_specs(layer + 1, mod_row_of_tile, (MOD_SHIFT, MOD_SCALE))
        in_specs += [pl.BlockSpec((1, 1, D_MODEL), lambda i: (layer + 1, 0, 0))]
        args += [mods, mods, norm_g]
        out_specs = [rows(D_MODEL), rows(D_MODEL)]
        out_shape = [stream, jax.ShapeDtypeStruct((t, D_MODEL), BF16)]
    return pl.pallas_call(
        functools.partial(_merge_kernel, final),
        grid=(t // tm,),
        in_specs=in_specs,
        out_specs=out_specs,
        out_shape=out_shape,
        compiler_params=pltpu.CompilerParams(
            dimension_semantics=("arbitrary",), vmem_limit_bytes=VMEM_LIMIT),
        name="merge",
    )(*args)


def _rope_tables(n_tokens):
    t = np.arange(n_tokens)
    pos = np.stack([t // GRID_W, t % GRID_W], axis=1).astype(np.float32)
    inv = (np.float32(ROPE_THETA) ** (-np.arange(ROPE_NFREQ, dtype=np.float32) / ROPE_NFREQ))
    ang = (pos[:, :, None] * inv[None, None, :]).astype(np.float32).astype(np.float64)
    cos, sin = np.cos(ang), np.sin(ang)
    zero = np.zeros_like(sin)
    c = np.stack([cos, cos], axis=2).reshape(n_tokens, HEAD_DIM)
    s1 = np.stack([-sin, zero], axis=2).reshape(n_tokens, HEAD_DIM)
    s2 = np.stack([zero, sin], axis=2).reshape(n_tokens, HEAD_DIM)
    return tuple(jnp.asarray(a.astype(np.float32)) for a in (c, s1, s2))


def _identity_tables(n_tokens):
    one = jnp.ones((n_tokens, HEAD_DIM), F32)
    zero = jnp.zeros((n_tokens, HEAD_DIM), F32)
    return one, zero, zero


def kernel(x, c, ctx, c_ctx, w_ada, b_ada, norm_g, w_in, q_norm_g, k_norm_g,
           w_proj_a, w_proj_b, w_out, final_g):
    batch, n_lat, d = x.shape
    n_ctx = ctx.shape[1]
    tm_norm, tm_merge, sub = 512, 256, 512
    ctx_row = batch

    cvec = jnp.concatenate(
        [c, c_ctx[None], jnp.zeros((MOD_ROWS - batch - 1, d), F32)], axis=0)
    mods = _ada(cvec, w_ada, b_ada).reshape(DEPTH * MOD_ROWS * 3, 1, d)

    rope = _rope_tables(n_lat)
    no_rope = _identity_tables(batch * n_ctx)
    heads_per_tile = COL_TILE // HEAD_DIM
    qg = jnp.tile(q_norm_g * Q_SCALE, (1, heads_per_tile)).reshape(DEPTH, 1, COL_TILE)
    kg = jnp.tile(k_norm_g, (1, heads_per_tile)).reshape(DEPTH, 1, COL_TILE)
    merge_w = (w_proj_a, w_proj_b, w_out)
    norm_g3 = norm_g.reshape(DEPTH, 1, d)
    fg = final_g.reshape(1, d)

    def lat_row(tm):
        return lambda i: i // (n_lat // tm)

    def ctx_mod_row(i):
        return ctx_row

    xs = x.reshape(batch * n_lat, d)
    cs = ctx.reshape(batch * n_ctx, d)
    h = _modnorm_call(xs, mods, 0, lat_row(tm_norm), norm_g3, tm_norm)
    hc = _modnorm_call(cs, mods, 0, ctx_mod_row, norm_g3, tm_norm)
    for l in range(DEPTH):
        last = l == DEPTH - 1
        lo, hi = (K_TILE, V_TILE + 1) if last else (0, N_COL_TILES)
        pc = _in_proj(hc, w_in, l, qg, kg, no_rope, lo, hi, batch * n_ctx, sub)
        p = _in_proj(h, w_in, l, qg, kg, rope, 0, N_COL_TILES, n_lat, sub)

        kv_sets = [(p["k"], p["vt"]), (pc["k"], pc["vt"])]
        if l == 0:
            attn, *cast = _attention(p["qt"], kv_sets, batch, 256, 512,
                                     casts=tuple(w.reshape(-1, d) for w in merge_w))
            w_pa, w_pb, w_o = (c.reshape(DEPTH, -1, d) for c in cast)
        else:
            attn = _attention(p["qt"], kv_sets, batch, 256, 512)
        fm = _fourier(p["rest"], batch)
        merged = _merge(p["rest"], attn, fm, xs, mods, l, lat_row(tm_merge), w_pa, w_pb, w_o,
                        norm_g3, fg, tm_merge)
        if last:
            xs = merged
        else:
            xs, h = merged
            attn_c = _attention(pc["qt"], [(pc["k"], pc["vt"])], batch, n_ctx, n_ctx)
            fm_c = _fourier(pc["rest"], batch)
            cs, hc = _merge(pc["rest"], attn_c, fm_c, cs, mods, l, ctx_mod_row, w_pa, w_pb, w_o,
                            norm_g3, fg, tm_merge)
    return xs.reshape(batch, n_lat, d)
```

```python
import functools
import math

import numpy as np
import jax
import jax.numpy as jnp
from jax import lax
from jax.experimental import pallas as pl
from jax.experimental.pallas import tpu as pltpu

F32 = jnp.float32
BF16 = jnp.bfloat16

D_MODEL = 2048
DEPTH = 2
CTX_LEN = 256
GRID_W = 64
HEAD_DIM = 128
ATT_W = (3 * D_MODEL) // 4
N_Q_HEADS = ATT_W // HEAD_DIM
N_KV_HEADS = 4
Q_PER_KV = N_Q_HEADS // N_KV_HEADS
KV_W = N_KV_HEADS * HEAD_DIM
FNET_W = D_MODEL // 4
N_FNET_GROUPS = 4
FNET_GROUP = FNET_W // N_FNET_GROUPS
ROPE_THETA = 10000.0
ROPE_NFREQ = HEAD_DIM // 4
EPS = 1e-6
IN_W = 2 * ATT_W + 2 * KV_W + 2 * FNET_W + 2 * D_MODEL
Q_SCALE = HEAD_DIM ** -0.5 * math.log2(math.e)

V7X_VMEM_BYTES = 64 * 1024 * 1024
VMEM_LIMIT = V7X_VMEM_BYTES - 8 * 1024 * 1024

COL_TILE = 512
N_COL_TILES = IN_W // COL_TILE
K_TILE = ATT_W // COL_TILE
V_TILE = K_TILE + 1
ZB_TILE = (2 * ATT_W + 2 * KV_W + FNET_W) // COL_TILE
REST_W = IN_W - ATT_W - 2 * KV_W
REST_ZA_BLOCK = 0
REST_UB_BLOCK = ATT_W // FNET_W
REST_GA_BLOCK = 1
REST_GB_BLOCK = 2
REST_ZB_BLOCK = REST_W // FNET_W - 1
MOD_ROWS = 8


def _silu(x):
    return x * jax.nn.sigmoid(x)


def _ada_kernel(cv_ref, w_ref, b_ref, o_ref):
    s = _silu(cv_ref[...]).astype(BF16)
    o_ref[0] = jnp.dot(s, w_ref[0].astype(BF16), preferred_element_type=F32) + b_ref[0]


def _ada(cvec, w_ada, b_ada):
    tn = 1024
    n = 3 * D_MODEL
    return pl.pallas_call(
        _ada_kernel,
        grid=(DEPTH, n // tn),
        in_specs=[
            pl.BlockSpec((MOD_ROWS, D_MODEL), lambda l, j: (0, 0)),
            pl.BlockSpec((1, D_MODEL, tn), lambda l, j: (l, 0, j)),
            pl.BlockSpec((1, 1, tn), lambda l, j: (l, 0, j)),
        ],
        out_specs=pl.BlockSpec((1, MOD_ROWS, tn), lambda l, j: (l, 0, j)),
        out_shape=jax.ShapeDtypeStruct((DEPTH, MOD_ROWS, n), F32),
        compiler_params=pltpu.CompilerParams(
            dimension_semantics=("arbitrary", "arbitrary"), vmem_limit_bytes=VMEM_LIMIT),
        name="ada",
    )(cvec, w_ada, b_ada.reshape(DEPTH, 1, n))


def _weight_tile(step):
    return jnp.where(step < ZB_TILE, step, jnp.where(step < N_COL_TILES - 1, step + 1, ZB_TILE))


def _head_norm_rope(acc, head_mean, g, c, s1, s2):
    ms = jnp.dot((acc * acc).astype(BF16), head_mean, preferred_element_type=F32)
    y = acc * lax.rsqrt(ms + EPS) * g
    outs = []
    for hh in range(COL_TILE // HEAD_DIM):
        yh = y[:, hh * HEAD_DIM:(hh + 1) * HEAD_DIM]
        outs.append(yh * c + pltpu.roll(yh, HEAD_DIM - ROPE_NFREQ, 1) * s1
                    + pltpu.roll(yh, ROPE_NFREQ, 1) * s2)
    return jnp.concatenate(outs, axis=-1)


def _modnorm(xf, g, scale, shift):
    ms = jnp.mean(xf * xf, axis=-1, keepdims=True)
    return xf * lax.rsqrt(ms + EPS) * g * (1.0 + scale) + shift


def _modnorm_kernel(x_ref, shift_ref, scale_ref, g_ref, h_ref):
    h_ref[...] = _modnorm(x_ref[...], g_ref[0], scale_ref[0], shift_ref[0]).astype(BF16)


def _mod_specs(layer, mod_row_of_tile, parts):
    return [pl.BlockSpec((1, 1, D_MODEL),
                         lambda i, p=p: ((layer * MOD_ROWS + mod_row_of_tile(i)) * 3 + p, 0, 0))
            for p in parts]


MOD_SHIFT, MOD_SCALE, MOD_GATE = 0, 1, 2


def _modnorm_call(x2d, mods, layer, mod_row_of_tile, norm_g, tm):
    t = x2d.shape[0]
    return pl.pallas_call(
        _modnorm_kernel,
        grid=(t // tm,),
        in_specs=[pl.BlockSpec((tm, D_MODEL), lambda i: (i, 0))]
        + _mod_specs(layer, mod_row_of_tile, (MOD_SHIFT, MOD_SCALE))
        + [pl.BlockSpec((1, 1, D_MODEL), lambda i: (layer, 0, 0))],
        out_specs=pl.BlockSpec((tm, D_MODEL), lambda i: (i, 0)),
        out_shape=jax.ShapeDtypeStruct((t, D_MODEL), BF16),
        compiler_params=pltpu.CompilerParams(
            dimension_semantics=("arbitrary",), vmem_limit_bytes=VMEM_LIMIT),
        name="modnorm",
    )(x2d, mods, mods, norm_g)


def _in_proj_kernel(step_lo, names, sub, h_ref, w_ref, qg_ref, kg_ref, hm_ref,
                    c_ref, s1_ref, s2_ref, *out_refs):
    out = dict(zip(names, out_refs))
    step = pl.program_id(1) + step_lo
    row_tiles = [slice(r, r + sub) for r in range(0, h_ref.shape[0], sub)]

    def tiles():
        w = w_ref[0].astype(BF16)
        for rows in row_tiles:
            yield rows, jnp.dot(h_ref[rows, :], w, preferred_element_type=F32)

    def head_tiles(g_ref):
        for rows, acc in tiles():
            yield rows, _head_norm_rope(acc, hm_ref[...], g_ref[0], c_ref[rows, :],
                                        s1_ref[rows, :], s2_ref[rows, :])

    if "qt" in out:
        @pl.when(step < K_TILE)
        def _():
            for rows, qh in head_tiles(qg_ref):
                out["qt"][:, rows] = qh.T.astype(BF16)

    if "k" in out:
        @pl.when(step == K_TILE)
        def _():
            for rows, kh in head_tiles(kg_ref):
                out["k"][rows, :] = kh.astype(BF16)

    if "vt" in out:
        @pl.when(step == V_TILE)
        def _():
            for rows, acc in tiles():
                out["vt"][:, rows] = acc.T.astype(BF16)

    if "rest" in out:
        @pl.when(step > V_TILE)
        def _():
            for rows, acc in tiles():
                out["rest"][rows, :] = acc.astype(BF16)


def _in_proj(h, w_in, layer, q_g, k_g, tables, step_lo, step_hi, group, sub):
    t = h.shape[0]
    c_tab, s1_tab, s2_tab = tables
    n_rest = REST_W // COL_TILE
    heads_per_tile = COL_TILE // HEAD_DIM
    head_mean = jnp.asarray(
        np.kron(np.eye(heads_per_tile), np.full((HEAD_DIM, HEAD_DIM), 1.0 / HEAD_DIM)), BF16)
    specs = {
        "qt": (pl.BlockSpec((COL_TILE, group),
                            lambda i, j: (jnp.clip(j + step_lo, 0, K_TILE - 1), i)),
               jax.ShapeDtypeStruct((ATT_W, t), BF16), step_lo < K_TILE),
        "k": (pl.BlockSpec((group, COL_TILE), lambda i, j: (i, 0)),
              jax.ShapeDtypeStruct((t, KV_W), BF16), step_lo <= K_TILE < step_hi),
        "vt": (pl.BlockSpec((COL_TILE, group), lambda i, j: (0, i)),
               jax.ShapeDtypeStruct((KV_W, t), BF16), step_lo <= V_TILE < step_hi),
        "rest": (pl.BlockSpec((group, COL_TILE),
                              lambda i, j: (i, jnp.clip(j + step_lo - V_TILE - 1, 0, n_rest - 1))),
                 jax.ShapeDtypeStruct((t, REST_W), BF16), step_hi > V_TILE + 1),
    }
    names = tuple(n for n, (_, _, present) in specs.items() if present)
    tab_spec = pl.BlockSpec((group, HEAD_DIM), lambda i, j: (0, 0))
    gain_spec = pl.BlockSpec((1, 1, COL_TILE), lambda i, j: (layer, 0, 0))
    outs = pl.pallas_call(
        functools.partial(_in_proj_kernel, step_lo, names, sub),
        grid=(t // group, step_hi - step_lo),
        in_specs=[
            pl.BlockSpec((group, D_MODEL), lambda i, j: (i, 0)),
            pl.BlockSpec((1, D_MODEL, COL_TILE),
                         lambda i, j: (layer, 0, _weight_tile(j + step_lo))),
            gain_spec, gain_spec,
            pl.BlockSpec((COL_TILE, COL_TILE), lambda i, j: (0, 0)),
            tab_spec, tab_spec, tab_spec,
        ],
        out_specs=[specs[n][0] for n in names],
        out_shape=[specs[n][1] for n in names],
        compiler_params=pltpu.CompilerParams(
            dimension_semantics=("arbitrary", "arbitrary"), vmem_limit_bytes=VMEM_LIMIT),
        name="in_proj",
    )(h, w_in, q_g, k_g, head_mean, c_tab, s1_tab, s2_tab)
    return dict(zip(names, outs))


ONES_ROWS = 16


def _col_reduce8(x, op):
    rows, w = x.shape
    return op(x.reshape(rows // 8, 8, w), axis=0)


def _attn_kernel(chunks, tq, n_casts, q_ref, *rest):
    n_sets = 1 + max(c[0] for c in chunks)
    k_refs, vt_refs = rest[:n_sets], rest[n_sets:2 * n_sets]
    cast_in = rest[2 * n_sets:2 * n_sets + n_casts]
    o_ref = rest[2 * n_sets + n_casts]
    cast_out = rest[2 * n_sets + n_casts + 1:2 * n_sets + 2 * n_casts + 1]
    s_scr, m_scr = rest[2 * n_sets + 2 * n_casts + 1:]
    width = Q_PER_KV * tq
    for src, dst in zip(cast_in, cast_out):
        dst[...] = pltpu.bitcast(src[...].astype(BF16), jnp.uint32)

    @pl.when(pl.program_id(0) == 0)
    def _():
        s_scr[...] = jnp.zeros_like(s_scr)
        m_scr[...] = jnp.zeros_like(m_scr)

    q_t = jnp.concatenate(
        [q_ref[i * HEAD_DIM:(i + 1) * HEAD_DIM, :] for i in range(Q_PER_KV)], axis=1)
    m_prev = m_scr[...]
    m_run = jnp.full((8, width), -jnp.inf, F32)
    acc = jnp.zeros((HEAD_DIM + ONES_ROWS, width), F32)
    for t, r0, rows, s0 in chunks:
        p = jnp.exp2(s_scr[s0:s0 + rows, :] - m_prev).astype(BF16)
        vt_ones = jnp.concatenate(
            [vt_refs[t][:, r0:r0 + rows], jnp.ones((ONES_ROWS, rows), BF16)], axis=0)
        acc = acc + jnp.dot(vt_ones, p, preferred_element_type=F32)
        s_new = jnp.dot(k_refs[t][r0:r0 + rows, :], q_t, preferred_element_type=F32)
        s_scr[s0:s0 + rows, :] = s_new
        m_run = jnp.maximum(m_run, _col_reduce8(s_new, jnp.max))
    m_scr[...] = m_run.max(axis=0, keepdims=True)
    o_t = acc[:HEAD_DIM] / acc[HEAD_DIM:HEAD_DIM + 1]
    for i in range(Q_PER_KV):
        o_ref[:, i * HEAD_DIM:(i + 1) * HEAD_DIM] = o_t[:, i * tq:(i + 1) * tq].T.astype(BF16)


CAST_STEPS = 64


def _attention(qt, kv_sets, batch, tq, key_chunk, casts=()):
    t = qt.shape[1]
    nq = t // batch // tq
    n_tiles = batch * N_KV_HEADS * nq
    gw = Q_PER_KV * HEAD_DIM
    assert not casts or n_tiles + 1 >= CAST_STEPS

    def cast_spec(rows):
        return pl.BlockSpec((rows // CAST_STEPS, D_MODEL),
                            lambda n: (jnp.minimum(n, CAST_STEPS - 1), 0))

    cast_in_specs = [cast_spec(a.shape[0]) for a in casts]
    cast_out_specs = [cast_spec(a.shape[0] // 2) for a in casts]
    cast_shapes = [jax.ShapeDtypeStruct((a.shape[0] // 2, a.shape[1]), jnp.uint32)
                   for a in casts]

    def tile(n):
        return n // (N_KV_HEADS * nq), (n // nq) % N_KV_HEADS, n % nq

    def score_tile(n):
        return tile(jnp.minimum(n, n_tiles - 1))

    def out_tile(n):
        return tile(jnp.maximum(n - 1, 0))

    def q_map(n):
        b, g, i = score_tile(n)
        return g, b * nq + i

    def o_map(n):
        b, g, i = out_tile(n)
        return b * nq + i, g

    def k_map(n):
        b, g, _ = score_tile(n)
        return b, g

    def vt_map(n):
        b, g, _ = out_tile(n)
        return g, b

    chunks, k_specs, vt_specs, s_rows = [], [], [], 0
    for s, (k, _) in enumerate(kv_sets):
        m_t = k.shape[0] // batch
        step = min(key_chunk, m_t)
        for r0 in range(0, m_t, step):
            chunks.append((s, r0, step, s_rows + r0))
        s_rows += m_t
        k_specs.append(pl.BlockSpec((m_t, HEAD_DIM), k_map))
        vt_specs.append(pl.BlockSpec((HEAD_DIM, m_t), vt_map))
    outs = pl.pallas_call(
        functools.partial(_attn_kernel, tuple(chunks), tq, len(casts)),
        grid=(n_tiles + 1,),
        in_specs=[pl.BlockSpec((gw, tq), q_map)] + k_specs + vt_specs + cast_in_specs,
        out_specs=[pl.BlockSpec((tq, gw), o_map)] + cast_out_specs,
        out_shape=[jax.ShapeDtypeStruct((t, ATT_W), BF16)] + cast_shapes,
        scratch_shapes=[pltpu.VMEM((s_rows, Q_PER_KV * tq), F32),
                        pltpu.VMEM((1, Q_PER_KV * tq), F32)],
        compiler_params=pltpu.CompilerParams(
            dimension_semantics=("arbitrary",), vmem_limit_bytes=VMEM_LIMIT),
        name="attn",
    )(qt, *[k for k, _ in kv_sets], *[vt for _, vt in kv_sets], *casts)
    return outs if casts else outs[0]


DFT_PAD_ROWS = 16


def _dft_tables(n):
    h = n // 2
    pos = np.arange(n, dtype=np.int64)
    ang = 2.0 * np.pi * ((pos[:h + 1, None] * pos[None, :]) % n).astype(np.float64) / n
    c_top = np.zeros((h + DFT_PAD_ROWS, n))
    c_top[:h + 1] = np.cos(ang) / np.sqrt(n)
    s_top = np.sin(ang[:h]) / np.sqrt(n)
    flip = np.zeros((h, h))
    flip[np.arange(h), (-np.arange(h)) % h] = 1.0
    ch = np.arange(FNET_GROUP, dtype=np.int64)
    ang_c = 2.0 * np.pi * ((ch[:, None] * ch[None, :]) % FNET_GROUP) / FNET_GROUP
    cc, sc = np.cos(ang_c) / np.sqrt(FNET_GROUP), np.sin(ang_c) / np.sqrt(FNET_GROUP)
    return tuple(a.astype(np.float32) for a in (c_top, s_top, flip, cc, sc))


def _fourier_kernel(n, u_ref, ct_ref, st_ref, flip_ref, cc_ref, sc_ref, o_ref, a_scr, b_scr):
    h = n // 2
    for g in range(N_FNET_GROUPS):
        cols = slice(g * FNET_GROUP, (g + 1) * FNET_GROUP)
        ug = u_ref[:, cols]
        a_scr[:, cols] = jnp.dot(ug, cc_ref[...], preferred_element_type=F32).astype(BF16)
        b_scr[:, cols] = jnp.dot(ug, sc_ref[...], preferred_element_type=F32).astype(BF16)
    p = jnp.dot(ct_ref[...], a_scr[...], preferred_element_type=F32)
    q = jnp.dot(st_ref[...], b_scr[...], preferred_element_type=F32)
    o_ref[0:h, :] = (p[:h] - q).astype(BF16)
    is_row0 = lax.broadcasted_iota(jnp.int32, (h, FNET_W), 0) == 0
    z = jnp.where(is_row0, p[h:h + 1], p[:h] + q).astype(BF16)
    o_ref[h:n, :] = jnp.dot(flip_ref[...], z, preferred_element_type=F32).astype(BF16)


def _fourier(rest, batch):
    t = rest.shape[0]
    n = t // batch
    tables = [jnp.asarray(a).astype(BF16) for a in _dft_tables(n)]

    def whole(a):
        return pl.BlockSpec(a.shape, lambda b: (0, 0), pipeline_mode=pl.Buffered(1))

    return pl.pallas_call(
        functools.partial(_fourier_kernel, n),
        grid=(batch,),
        in_specs=[pl.BlockSpec((n, FNET_W), lambda b: (b, REST_UB_BLOCK))]
        + [whole(a) for a in tables],
        out_specs=pl.BlockSpec((n, FNET_W), lambda b: (b, 0)),
        out_shape=jax.ShapeDtypeStruct((t, FNET_W), BF16),
        scratch_shapes=[pltpu.VMEM((n, FNET_W), BF16), pltpu.VMEM((n, FNET_W), BF16)],
        compiler_params=pltpu.CompilerParams(
            dimension_semantics=("arbitrary",), vmem_limit_bytes=VMEM_LIMIT),
        name="fourier",
    )(rest, *tables)


def _merge_kernel(final, attn_ref, za_ref, fm_ref, zb_ref, ga_ref, gb_ref, x_ref, gate_ref,
                  wpa_ref, wpb_ref, wo_ref, *rest):
    def weight(w_ref):
        return pltpu.bitcast(w_ref[0], BF16)

    ta = (attn_ref[...].astype(F32) * _silu(za_ref[...].astype(F32))).astype(BF16)
    ya = jnp.dot(ta, weight(wpa_ref), preferred_element_type=F32)
    tb = (fm_ref[...].astype(F32) * _silu(zb_ref[...].astype(F32))).astype(BF16)
    yb = jnp.dot(tb, weight(wpb_ref), preferred_element_type=F32)
    mix = (jax.nn.sigmoid(ga_ref[...].astype(F32)) * ya
           + jax.nn.sigmoid(gb_ref[...].astype(F32)) * yb).astype(BF16)
    y = jnp.dot(mix, weight(wo_ref), preferred_element_type=F32)
    out = x_ref[...] + gate_ref[0] * y
    if final:
        fg_ref, o_ref = rest
        ms = jnp.mean(out * out, axis=-1, keepdims=True)
        o_ref[...] = out * lax.rsqrt(ms + EPS) * fg_ref[...]
    else:
        shift_ref, scale_ref, g_ref, o_ref, h_ref = rest
        o_ref[...] = out
        h_ref[...] = _modnorm(out, g_ref[0], scale_ref[0], shift_ref[0]).astype(BF16)


def _merge(rest, attn, fm, x2d, mods, layer, mod_row_of_tile, w_pa, w_pb, w_o, norm_g, final_g,
           tm):
    t = x2d.shape[0]
    final = layer == DEPTH - 1

    def rows(width, block=0):
        return pl.BlockSpec((tm, width), lambda i: (i, block))

    def weight(k):
        return pl.BlockSpec((1, k // 2, D_MODEL), lambda i: (layer, 0, 0),
                            pipeline_mode=pl.Buffered(1))

    in_specs = [
        rows(ATT_W), rows(ATT_W, REST_ZA_BLOCK), rows(FNET_W), rows(FNET_W, REST_ZB_BLOCK),
        rows(D_MODEL, REST_GA_BLOCK), rows(D_MODEL, REST_GB_BLOCK),
        rows(D_MODEL),
    ] + _mod_specs(layer, mod_row_of_tile, (MOD_GATE,)) + [
        weight(ATT_W), weight(FNET_W), weight(D_MODEL)]
    args = [attn, rest, fm, rest, rest, rest, x2d, mods, w_pa, w_pb, w_o]
    stream = jax.ShapeDtypeStruct((t, D_MODEL), F32)
    if final:
        in_specs += [pl.BlockSpec((1, D_MODEL), lambda i: (0, 0))]
        args += [final_g]
        out_specs, out_shape = rows(D_MODEL), stream
    else:
        in_specs += _mod_specs(layer + 1, mod_row_of_tile, (MOD_SHIFT, MOD_SCALE))
        in_specs += [pl.BlockSpec((1, 1, D_MODEL), lambda i: (layer + 1, 0, 0))]
        args += [mods, mods, norm_g]
        out_specs = [rows(D_MODEL), rows(D_MODEL)]
        out_shape = [stream, jax.ShapeDtypeStruct((t, D_MODEL), BF16)]
    return pl.pallas_call(
        functools.partial(_merge_kernel, final),
        grid=(t // tm,),
        in_specs=in_specs,
        out_specs=out_specs,
        out_shape=out_shape,
        compiler_params=pltpu.CompilerParams(
            dimension_semantics=("arbitrary",), vmem_limit_bytes=VMEM_LIMIT),
        name="merge",
    )(*args)


def _rope_tables(n_tokens):
    t = np.arange(n_tokens)
    pos = np.stack([t // GRID_W, t % GRID_W], axis=1).astype(np.float32)
    inv = (np.float32(ROPE_THETA) ** (-np.arange(ROPE_NFREQ, dtype=np.float32) / ROPE_NFREQ))
    ang = (pos[:, :, None] * inv[None, None, :]).astype(np.float32).astype(np.float64)
    cos, sin = np.cos(ang), np.sin(ang)
    zero = np.zeros_like(sin)
    c = np.stack([cos, cos], axis=2).reshape(n_tokens, HEAD_DIM)
    s1 = np.stack([-sin, zero], axis=2).reshape(n_tokens, HEAD_DIM)
    s2 = np.stack([zero, sin], axis=2).reshape(n_tokens, HEAD_DIM)
    return tuple(jnp.asarray(a.astype(np.float32)) for a in (c, s1, s2))


def _identity_tables(n_tokens):
    one = jnp.ones((n_tokens, HEAD_DIM), F32)
    zero = jnp.zeros((n_tokens, HEAD_DIM), F32)
    return one, zero, zero


def kernel(x, c, ctx, c_ctx, w_ada, b_ada, norm_g, w_in, q_norm_g, k_norm_g,
           w_proj_a, w_proj_b, w_out, final_g):
    batch, n_lat, d = x.shape
    n_ctx = ctx.shape[1]
    tm_norm, tm_merge, sub = 1024, 256, 512
    ctx_row = batch

    cvec = jnp.concatenate(
        [c, c_ctx[None], jnp.zeros((MOD_ROWS - batch - 1, d), F32)], axis=0)
    mods = _ada(cvec, w_ada, b_ada).reshape(DEPTH * MOD_ROWS * 3, 1, d)

    rope = _rope_tables(n_lat)
    no_rope = _identity_tables(batch * n_ctx)
    heads_per_tile = COL_TILE // HEAD_DIM
    qg = jnp.tile(q_norm_g * Q_SCALE, (1, heads_per_tile)).reshape(DEPTH, 1, COL_TILE)
    kg = jnp.tile(k_norm_g, (1, heads_per_tile)).reshape(DEPTH, 1, COL_TILE)
    merge_w = (w_proj_a, w_proj_b, w_out)
    norm_g3 = norm_g.reshape(DEPTH, 1, d)
    fg = final_g.reshape(1, d)

    def lat_row(tm):
        return lambda i: i // (n_lat // tm)

    def ctx_mod_row(i):
        return ctx_row

    xs = x.reshape(batch * n_lat, d)
    cs = ctx.reshape(batch * n_ctx, d)
    h = _modnorm_call(xs, mods, 0, lat_row(tm_norm), norm_g3, tm_norm)
    hc = _modnorm_call(cs, mods, 0, ctx_mod_row, norm_g3, tm_norm)
    for l in range(DEPTH):
        last = l == DEPTH - 1
        lo, hi = (K_TILE, V_TILE + 1) if last else (0, N_COL_TILES)
        pc = _in_proj(hc, w_in, l, qg, kg, no_rope, lo, hi, batch * n_ctx, sub)
        p = _in_proj(h, w_in, l, qg, kg, rope, 0, N_COL_TILES, n_lat, sub)

        kv_sets = [(p["k"], p["vt"]), (pc["k"], pc["vt"])]
        if l == 0:
            attn, *cast = _attention(p["qt"], kv_sets, batch, 256, 256,
                                     casts=tuple(w.reshape(-1, d) for w in merge_w))
            w_pa, w_pb, w_o = (c.reshape(DEPTH, -1, d) for c in cast)
        else:
            attn = _attention(p["qt"], kv_sets, batch, 256, 256)
        fm = _fourier(p["rest"], batch)
        merged = _merge(p["rest"], attn, fm, xs, mods, l, lat_row(tm_merge), w_pa, w_pb, w_o,
                        norm_g3, fg, tm_merge)
        if last:
            xs = merged
        else:
            xs, h = merged
            attn_c = _attention(pc["qt"], [(pc["k"], pc["vt"])], batch, n_ctx, n_ctx)
            fm_c = _fourier(pc["rest"], batch)
            cs, hc = _merge(pc["rest"], attn_c, fm_c, cs, mods, l, ctx_mod_row, w_pa, w_pb, w_o,
                            norm_g3, fg, tm_merge)
    return xs.reshape(batch, n_lat, d)
```

```python
import functools
import math

import numpy as np
import jax
import jax.numpy as jnp
from jax import lax
from jax.experimental import pallas as pl
from jax.experimental.pallas import tpu as pltpu

F32 = jnp.float32
BF16 = jnp.bfloat16

D_MODEL = 2048
DEPTH = 2
CTX_LEN = 256
GRID_W = 64
HEAD_DIM = 128
ATT_W = (3 * D_MODEL) // 4
N_Q_HEADS = ATT_W // HEAD_DIM
N_KV_HEADS = 4
Q_PER_KV = N_Q_HEADS // N_KV_HEADS
KV_W = N_KV_HEADS * HEAD_DIM
FNET_W = D_MODEL // 4
N_FNET_GROUPS = 4
FNET_GROUP = FNET_W // N_FNET_GROUPS
ROPE_THETA = 10000.0
ROPE_NFREQ = HEAD_DIM // 4
EPS = 1e-6
IN_W = 2 * ATT_W + 2 * KV_W + 2 * FNET_W + 2 * D_MODEL
Q_SCALE = HEAD_DIM ** -0.5 * math.log2(math.e)

V7X_VMEM_BYTES = 64 * 1024 * 1024
V7X_VMEM_RESERVE_BYTES = 8 * 1024 * 1024
VMEM_LIMIT = V7X_VMEM_BYTES - V7X_VMEM_RESERVE_BYTES

COL_TILE = 512
N_COL_TILES = IN_W // COL_TILE
K_TILE = ATT_W // COL_TILE
V_TILE = K_TILE + 1
ZB_TILE = (2 * ATT_W + 2 * KV_W + FNET_W) // COL_TILE
REST_W = IN_W - ATT_W - 2 * KV_W
REST_ZA_BLOCK = 0
REST_UB_BLOCK = ATT_W // FNET_W
REST_GA_BLOCK = 1
REST_GB_BLOCK = 2
REST_ZB_BLOCK = REST_W // FNET_W - 1
MOD_ROWS = 8


def _silu(x):
    return x * jax.nn.sigmoid(x)


def _ada_kernel(cv_ref, w_ref, b_ref, o_ref):
    s = _silu(cv_ref[...]).astype(BF16)
    o_ref[0] = jnp.dot(s, w_ref[0].astype(BF16), preferred_element_type=F32) + b_ref[0]


def _ada(cvec, w_ada, b_ada):
    tn = 1024
    n = 3 * D_MODEL
    return pl.pallas_call(
        _ada_kernel,
        grid=(DEPTH, n // tn),
        in_specs=[
            pl.BlockSpec((MOD_ROWS, D_MODEL), lambda l, j: (0, 0)),
            pl.BlockSpec((1, D_MODEL, tn), lambda l, j: (l, 0, j)),
            pl.BlockSpec((1, 1, tn), lambda l, j: (l, 0, j)),
        ],
        out_specs=pl.BlockSpec((1, MOD_ROWS, tn), lambda l, j: (l, 0, j)),
        out_shape=jax.ShapeDtypeStruct((DEPTH, MOD_ROWS, n), F32),
        compiler_params=pltpu.CompilerParams(
            dimension_semantics=("arbitrary", "arbitrary"), vmem_limit_bytes=VMEM_LIMIT),
        name="ada",
    )(cvec, w_ada, b_ada.reshape(DEPTH, 1, n))


def _weight_tile(step):
    return jnp.where(step < ZB_TILE, step, jnp.where(step < N_COL_TILES - 1, step + 1, ZB_TILE))


def _head_norm_rope(acc, head_mean, g, c, s1, s2):
    ms = jnp.dot((acc * acc).astype(BF16), head_mean, preferred_element_type=F32)
    y = acc * lax.rsqrt(ms + EPS) * g
    outs = []
    for hh in range(COL_TILE // HEAD_DIM):
        yh = y[:, hh * HEAD_DIM:(hh + 1) * HEAD_DIM]
        outs.append(yh * c + pltpu.roll(yh, HEAD_DIM - ROPE_NFREQ, 1) * s1
                    + pltpu.roll(yh, ROPE_NFREQ, 1) * s2)
    return jnp.concatenate(outs, axis=-1)


def _modnorm(xf, g, scale, shift):
    ms = jnp.mean(xf * xf, axis=-1, keepdims=True)
    return xf * lax.rsqrt(ms + EPS) * g * (1.0 + scale) + shift


def _modnorm_kernel(x_ref, shift_ref, scale_ref, g_ref, h_ref):
    h_ref[...] = _modnorm(x_ref[...], g_ref[0], scale_ref[0], shift_ref[0]).astype(BF16)


def _mod_specs(layer, mod_row_of_tile, parts):
    return [pl.BlockSpec((1, 1, D_MODEL),
                         lambda i, p=p: ((layer * MOD_ROWS + mod_row_of_tile(i)) * 3 + p, 0, 0))
            for p in parts]


MOD_SHIFT, MOD_SCALE, MOD_GATE = 0, 1, 2


def _modnorm_call(x2d, mods, layer, mod_row_of_tile, norm_g, tm):
    t = x2d.shape[0]
    return pl.pallas_call(
        _modnorm_kernel,
        grid=(t // tm,),
        in_specs=[pl.BlockSpec((tm, D_MODEL), lambda i: (i, 0))]
        + _mod_specs(layer, mod_row_of_tile, (MOD_SHIFT, MOD_SCALE))
        + [pl.BlockSpec((1, 1, D_MODEL), lambda i: (layer, 0, 0))],
        out_specs=pl.BlockSpec((tm, D_MODEL), lambda i: (i, 0)),
        out_shape=jax.ShapeDtypeStruct((t, D_MODEL), BF16),
        compiler_params=pltpu.CompilerParams(
            dimension_semantics=("arbitrary",), vmem_limit_bytes=VMEM_LIMIT),
        name="modnorm",
    )(x2d, mods, mods, norm_g)


def _pack_bf16_rows(x):
    return pltpu.bitcast(x.astype(BF16), jnp.uint32)


def _in_proj_kernel(step_lo, names, sub, n_casts, h_ref, w_ref, qg_ref, kg_ref, hm_ref,
                    c_ref, s1_ref, s2_ref, *rest):
    cast_in, out_refs = rest[:n_casts], rest[n_casts:len(rest) - n_casts]
    cast_out = rest[len(rest) - n_casts:]
    out = dict(zip(names, out_refs))
    step = pl.program_id(1) + step_lo
    row_tiles = [slice(r, r + sub) for r in range(0, h_ref.shape[0], sub)]

    def tiles():
        for src, dst in zip(cast_in, cast_out):
            dst[...] = _pack_bf16_rows(src[...])
        w = w_ref[0].astype(BF16)
        for rows in row_tiles:
            yield rows, jnp.dot(h_ref[rows, :], w, preferred_element_type=F32)

    def head_tiles(g_ref):
        for rows, acc in tiles():
            yield rows, _head_norm_rope(acc, hm_ref[...], g_ref[0], c_ref[rows, :],
                                        s1_ref[rows, :], s2_ref[rows, :])

    if "qt" in out:
        @pl.when(step < K_TILE)
        def _():
            for rows, qh in head_tiles(qg_ref):
                out["qt"][:, rows] = qh.T.astype(BF16)

    if "k" in out:
        @pl.when(step == K_TILE)
        def _():
            for rows, kh in head_tiles(kg_ref):
                out["k"][rows, :] = kh.astype(BF16)

    if "vt" in out:
        @pl.when(step == V_TILE)
        def _():
            for rows, acc in tiles():
                out["vt"][:, rows] = acc.T.astype(BF16)

    if "rest" in out:
        @pl.when(step > V_TILE)
        def _():
            for rows, acc in tiles():
                out["rest"][rows, :] = acc.astype(BF16)


CAST_STEPS = 64


def _in_proj(h, w_in, layer, q_g, k_g, tables, step_lo, step_hi, group, sub, casts=()):
    t = h.shape[0]
    n_steps = step_hi - step_lo
    assert not casts or (t // group) * n_steps >= CAST_STEPS

    def cast_spec(rows):
        return pl.BlockSpec((rows // CAST_STEPS, D_MODEL),
                            lambda i, j: (jnp.minimum(i * n_steps + j, CAST_STEPS - 1), 0))

    cast_in_specs = [cast_spec(a.shape[0]) for a in casts]
    cast_out_specs = [cast_spec(a.shape[0] // 2) for a in casts]
    cast_shapes = [jax.ShapeDtypeStruct((a.shape[0] // 2, a.shape[1]), jnp.uint32)
                   for a in casts]
    c_tab, s1_tab, s2_tab = tables
    n_rest = REST_W // COL_TILE
    heads_per_tile = COL_TILE // HEAD_DIM
    head_mean = jnp.asarray(
        np.kron(np.eye(heads_per_tile), np.full((HEAD_DIM, HEAD_DIM), 1.0 / HEAD_DIM)), BF16)
    specs = {
        "qt": (pl.BlockSpec((COL_TILE, group),
                            lambda i, j: (jnp.clip(j + step_lo, 0, K_TILE - 1), i)),
               jax.ShapeDtypeStruct((ATT_W, t), BF16), step_lo < K_TILE),
        "k": (pl.BlockSpec((group, COL_TILE), lambda i, j: (i, 0)),
              jax.ShapeDtypeStruct((t, KV_W), BF16), step_lo <= K_TILE < step_hi),
        "vt": (pl.BlockSpec((COL_TILE, group), lambda i, j: (0, i)),
               jax.ShapeDtypeStruct((KV_W, t), BF16), step_lo <= V_TILE < step_hi),
        "rest": (pl.BlockSpec((group, COL_TILE),
                              lambda i, j: (i, jnp.clip(j + step_lo - V_TILE - 1, 0, n_rest - 1))),
                 jax.ShapeDtypeStruct((t, REST_W), BF16), step_hi > V_TILE + 1),
    }
    names = tuple(n for n, (_, _, present) in specs.items() if present)
    tab_spec = pl.BlockSpec((group, HEAD_DIM), lambda i, j: (0, 0))
    gain_spec = pl.BlockSpec((1, 1, COL_TILE), lambda i, j: (layer, 0, 0))
    outs = pl.pallas_call(
        functools.partial(_in_proj_kernel, step_lo, names, sub, len(casts)),
        grid=(t // group, n_steps),
        in_specs=[
            pl.BlockSpec((group, D_MODEL), lambda i, j: (i, 0)),
            pl.BlockSpec((1, D_MODEL, COL_TILE),
                         lambda i, j: (layer, 0, _weight_tile(j + step_lo))),
            gain_spec, gain_spec,
            pl.BlockSpec((COL_TILE, COL_TILE), lambda i, j: (0, 0)),
            tab_spec, tab_spec, tab_spec,
        ] + cast_in_specs,
        out_specs=[specs[n][0] for n in names] + cast_out_specs,
        out_shape=[specs[n][1] for n in names] + cast_shapes,
        compiler_params=pltpu.CompilerParams(
            dimension_semantics=("arbitrary", "arbitrary"), vmem_limit_bytes=VMEM_LIMIT),
        name="in_proj",
    )(h, w_in, q_g, k_g, head_mean, c_tab, s1_tab, s2_tab, *casts)
    result = dict(zip(names, outs))
    if casts:
        result["casts"] = outs[len(names):]
    return result


ONES_ROWS = 16


def _col_reduce8(x, op):
    rows, w = x.shape
    return op(x.reshape(rows // 8, 8, w), axis=0)


def _attn_kernel(chunks, tq, q_ref, *rest):
    n_sets = 1 + max(c[0] for c in chunks)
    k_refs, vt_refs = rest[:n_sets], rest[n_sets:2 * n_sets]
    o_ref, s_scr, m_scr = rest[2 * n_sets:]
    width = Q_PER_KV * tq

    @pl.when(pl.program_id(0) == 0)
    def _():
        s_scr[...] = jnp.zeros_like(s_scr)
        m_scr[...] = jnp.zeros_like(m_scr)

    q_t = jnp.concatenate(
        [q_ref[i * HEAD_DIM:(i + 1) * HEAD_DIM, :] for i in range(Q_PER_KV)], axis=1)
    m_prev = m_scr[...]
    m_run = jnp.full((8, width), -jnp.inf, F32)
    acc = jnp.zeros((HEAD_DIM + ONES_ROWS, width), F32)
    for t, r0, rows, s0 in chunks:
        p = jnp.exp2(s_scr[s0:s0 + rows, :] - m_prev).astype(BF16)
        vt_ones = jnp.concatenate(
            [vt_refs[t][:, r0:r0 + rows], jnp.ones((ONES_ROWS, rows), BF16)], axis=0)
        acc = acc + jnp.dot(vt_ones, p, preferred_element_type=F32)
        s_new = jnp.dot(k_refs[t][r0:r0 + rows, :], q_t, preferred_element_type=F32)
        s_scr[s0:s0 + rows, :] = s_new
        m_run = jnp.maximum(m_run, _col_reduce8(s_new, jnp.max))
    m_scr[...] = m_run.max(axis=0, keepdims=True)
    o_t = acc[:HEAD_DIM] / acc[HEAD_DIM:HEAD_DIM + 1]
    for i in range(Q_PER_KV):
        o_ref[:, i * HEAD_DIM:(i + 1) * HEAD_DIM] = o_t[:, i * tq:(i + 1) * tq].T.astype(BF16)


def _attention(qt, kv_sets, batch, tq, key_chunk):
    t = qt.shape[1]
    nq = t // batch // tq
    n_tiles = batch * N_KV_HEADS * nq
    gw = Q_PER_KV * HEAD_DIM

    def tile(n):
        return n // (N_KV_HEADS * nq), (n // nq) % N_KV_HEADS, n % nq

    def score_tile(n):
        return tile(jnp.minimum(n, n_tiles - 1))

    def out_tile(n):
        return tile(jnp.maximum(n - 1, 0))

    def q_map(n):
        b, g, i = score_tile(n)
        return g, b * nq + i

    def o_map(n):
        b, g, i = out_tile(n)
        return b * nq + i, g

    def k_map(n):
        b, g, _ = score_tile(n)
        return b, g

    def vt_map(n):
        b, g, _ = out_tile(n)
        return g, b

    chunks, k_specs, vt_specs, s_rows = [], [], [], 0
    for s, (k, _) in enumerate(kv_sets):
        m_t = k.shape[0] // batch
        step = min(key_chunk, m_t)
        for r0 in range(0, m_t, step):
            chunks.append((s, r0, step, s_rows + r0))
        s_rows += m_t
        k_specs.append(pl.BlockSpec((m_t, HEAD_DIM), k_map))
        vt_specs.append(pl.BlockSpec((HEAD_DIM, m_t), vt_map))
    return pl.pallas_call(
        functools.partial(_attn_kernel, tuple(chunks), tq),
        grid=(n_tiles + 1,),
        in_specs=[pl.BlockSpec((gw, tq), q_map)] + k_specs + vt_specs,
        out_specs=pl.BlockSpec((tq, gw), o_map),
        out_shape=jax.ShapeDtypeStruct((t, ATT_W), BF16),
        scratch_shapes=[pltpu.VMEM((s_rows, Q_PER_KV * tq), F32),
                        pltpu.VMEM((1, Q_PER_KV * tq), F32)],
        compiler_params=pltpu.CompilerParams(
            dimension_semantics=("arbitrary",), vmem_limit_bytes=VMEM_LIMIT),
        name="attn",
    )(qt, *[k for k, _ in kv_sets], *[vt for _, vt in kv_sets])


DFT_PAD_ROWS = 16


def _dft_tables(n):
    h = n // 2
    pos = np.arange(n, dtype=np.int64)
    ang = 2.0 * np.pi * ((pos[:h + 1, None] * pos[None, :]) % n).astype(np.float64) / n
    c_top = np.zeros((h + DFT_PAD_ROWS, n))
    c_top[:h + 1] = np.cos(ang) / np.sqrt(n)
    s_top = np.sin(ang[:h]) / np.sqrt(n)
    flip = np.zeros((h, h))
    flip[np.arange(h), (-np.arange(h)) % h] = 1.0
    ch = np.arange(FNET_GROUP, dtype=np.int64)
    ang_c = 2.0 * np.pi * ((ch[:, None] * ch[None, :]) % FNET_GROUP) / FNET_GROUP
    cc, sc = np.cos(ang_c) / np.sqrt(FNET_GROUP), np.sin(ang_c) / np.sqrt(FNET_GROUP)
    return tuple(a.astype(np.float32) for a in (c_top, s_top, flip, cc, sc))


def _fourier_kernel(n, u_ref, ct_ref, st_ref, flip_ref, cc_ref, sc_ref, o_ref, a_scr, b_scr):
    h = n // 2
    for g in range(N_FNET_GROUPS):
        cols = slice(g * FNET_GROUP, (g + 1) * FNET_GROUP)
        ug = u_ref[:, cols]
        a_scr[:, cols] = jnp.dot(ug, cc_ref[...], preferred_element_type=F32).astype(BF16)
        b_scr[:, cols] = jnp.dot(ug, sc_ref[...], preferred_element_type=F32).astype(BF16)
    p = jnp.dot(ct_ref[...], a_scr[...], preferred_element_type=F32)
    q = jnp.dot(st_ref[...], b_scr[...], preferred_element_type=F32)
    o_ref[0:h, :] = (p[:h] - q).astype(BF16)
    is_row0 = lax.broadcasted_iota(jnp.int32, (h, FNET_W), 0) == 0
    z = jnp.where(is_row0, p[h:h + 1], p[:h] + q).astype(BF16)
    o_ref[h:n, :] = jnp.dot(flip_ref[...], z, preferred_element_type=F32).astype(BF16)


def _fourier(rest, batch):
    t = rest.shape[0]
    n = t // batch
    tables = [jnp.asarray(a).astype(BF16) for a in _dft_tables(n)]

    def whole(a):
        return pl.BlockSpec(a.shape, lambda b: (0, 0), pipeline_mode=pl.Buffered(1))

    return pl.pallas_call(
        functools.partial(_fourier_kernel, n),
        grid=(batch,),
        in_specs=[pl.BlockSpec((n, FNET_W), lambda b: (b, REST_UB_BLOCK))]
        + [whole(a) for a in tables],
        out_specs=pl.BlockSpec((n, FNET_W), lambda b: (b, 0)),
        out_shape=jax.ShapeDtypeStruct((t, FNET_W), BF16),
        scratch_shapes=[pltpu.VMEM((n, FNET_W), BF16), pltpu.VMEM((n, FNET_W), BF16)],
        compiler_params=pltpu.CompilerParams(
            dimension_semantics=("arbitrary",), vmem_limit_bytes=VMEM_LIMIT),
        name="fourier",
    )(rest, *tables)


def _merge_kernel(final, attn_ref, za_ref, fm_ref, zb_ref, ga_ref, gb_ref, x_ref, gate_ref,
                  wpa_ref, wpb_ref, wo_ref, *rest):
    def weight(w_ref):
        return pltpu.bitcast(w_ref[0], BF16)

    ta = (attn_ref[...].astype(F32) * _silu(za_ref[...].astype(F32))).astype(BF16)
    ya = jnp.dot(ta, weight(wpa_ref), preferred_element_type=F32)
    tb = (fm_ref[...].astype(F32) * _silu(zb_ref[...].astype(F32))).astype(BF16)
    yb = jnp.dot(tb, weight(wpb_ref), preferred_element_type=F32)
    mix = (jax.nn.sigmoid(ga_ref[...].astype(F32)) * ya
           + jax.nn.sigmoid(gb_ref[...].astype(F32)) * yb).astype(BF16)
    y = jnp.dot(mix, weight(wo_ref), preferred_element_type=F32)
    out = x_ref[...] + gate_ref[0] * y
    if final:
        fg_ref, o_ref = rest
        ms = jnp.mean(out * out, axis=-1, keepdims=True)
        o_ref[...] = out * lax.rsqrt(ms + EPS) * fg_ref[...]
    else:
        shift_ref, scale_ref, g_ref, o_ref, h_ref = rest
        o_ref[...] = out
        h_ref[...] = _modnorm(out, g_ref[0], scale_ref[0], shift_ref[0]).astype(BF16)


def _merge(rest, attn, fm, x2d, mods, layer, mod_row_of_tile, w_pa, w_pb, w_o, norm_g, final_g,
           tm):
    t = x2d.shape[0]
    final = layer == DEPTH - 1

    def rows(width, block=0):
        return pl.BlockSpec((tm, width), lambda i: (i, block))

    def weight(k):
        return pl.BlockSpec((1, k // 2, D_MODEL), lambda i: (layer, 0, 0),
                            pipeline_mode=pl.Buffered(1))

    in_specs = [
        rows(ATT_W), rows(ATT_W, REST_ZA_BLOCK), rows(FNET_W), rows(FNET_W, REST_ZB_BLOCK),
        rows(D_MODEL, REST_GA_BLOCK), rows(D_MODEL, REST_GB_BLOCK),
        rows(D_MODEL),
    ] + _mod_specs(layer, mod_row_of_tile, (MOD_GATE,)) + [
        weight(ATT_W), weight(FNET_W), weight(D_MODEL)]
    args = [attn, rest, fm, rest, rest, rest, x2d, mods, w_pa, w_pb, w_o]
    stream = jax.ShapeDtypeStruct((t, D_MODEL), F32)
    if final:
        in_specs += [pl.BlockSpec((1, D_MODEL), lambda i: (0, 0))]
        args += [final_g]
        out_specs, out_shape = rows(D_MODEL), stream
    else:
        in_specs += _mod_specs(layer + 1, mod_row_of_tile, (MOD_SHIFT, MOD_SCALE))
        in_specs += [pl.BlockSpec((1, 1, D_MODEL), lambda i: (layer + 1, 0, 0))]
        args += [mods, mods, norm_g]
        out_specs = [rows(D_MODEL), rows(D_MODEL)]
        out_shape = [stream, jax.ShapeDtypeStruct((t, D_MODEL), BF16)]
    return pl.pallas_call(
        functools.partial(_merge_kernel, final),
        grid=(t // tm,),
        in_specs=in_specs,
        out_specs=out_specs,
        out_shape=out_shape,
        compiler_params=pltpu.CompilerParams(
            dimension_semantics=("arbitrary",), vmem_limit_bytes=VMEM_LIMIT),
        name="merge",
    )(*args)


def _rope_tables(n_tokens):
    t = np.arange(n_tokens)
    pos = np.stack([t // GRID_W, t % GRID_W], axis=1).astype(np.float32)
    inv = (np.float32(ROPE_THETA) ** (-np.arange(ROPE_NFREQ, dtype=np.float32) / ROPE_NFREQ))
    ang = (pos[:, :, None] * inv[None, None, :]).astype(np.float32).astype(np.float64)
    cos, sin = np.cos(ang), np.sin(ang)
    zero = np.zeros_like(sin)
    c = np.stack([cos, cos], axis=2).reshape(n_tokens, HEAD_DIM)
    s1 = np.stack([-sin, zero], axis=2).reshape(n_tokens, HEAD_DIM)
    s2 = np.stack([zero, sin], axis=2).reshape(n_tokens, HEAD_DIM)
    return tuple(jnp.asarray(a.astype(np.float32)) for a in (c, s1, s2))


def _identity_tables(n_tokens):
    one = jnp.ones((n_tokens, HEAD_DIM), F32)
    zero = jnp.zeros((n_tokens, HEAD_DIM), F32)
    return one, zero, zero


def kernel(x, c, ctx, c_ctx, w_ada, b_ada, norm_g, w_in, q_norm_g, k_norm_g,
           w_proj_a, w_proj_b, w_out, final_g):
    batch, n_lat, d = x.shape
    n_ctx = ctx.shape[1]
    tm_norm, tm_merge, sub = 1024, 256, 512
    ctx_row = batch

    cvec = jnp.concatenate(
        [c, c_ctx[None], jnp.zeros((MOD_ROWS - batch - 1, d), F32)], axis=0)
    mods = _ada(cvec, w_ada, b_ada).reshape(DEPTH * MOD_ROWS * 3, 1, d)

    rope = _rope_tables(n_lat)
    no_rope = _identity_tables(batch * n_ctx)
    heads_per_tile = COL_TILE // HEAD_DIM
    qg = jnp.tile(q_norm_g * Q_SCALE, (1, heads_per_tile)).reshape(DEPTH, 1, COL_TILE)
    kg = jnp.tile(k_norm_g, (1, heads_per_tile)).reshape(DEPTH, 1, COL_TILE)
    merge_w = (w_proj_a, w_proj_b, w_out)
    norm_g3 = norm_g.reshape(DEPTH, 1, d)
    fg = final_g.reshape(1, d)

    def lat_row(tm):
        return lambda i: i // (n_lat // tm)

    def ctx_mod_row(i):
        return ctx_row

    xs = x.reshape(batch * n_lat, d)
    cs = ctx.reshape(batch * n_ctx, d)
    h = _modnorm_call(xs, mods, 0, lat_row(tm_norm), norm_g3, tm_norm)
    hc = _modnorm_call(cs, mods, 0, ctx_mod_row, norm_g3, tm_norm)
    for l in range(DEPTH):
        last = l == DEPTH - 1
        lo, hi = (K_TILE, V_TILE + 1) if last else (0, N_COL_TILES)
        pc = _in_proj(hc, w_in, l, qg, kg, no_rope, lo, hi, batch * n_ctx, sub)
        casts = tuple(w.reshape(-1, d) for w in merge_w) if l == 0 else ()
        p = _in_proj(h, w_in, l, qg, kg, rope, 0, N_COL_TILES, n_lat, sub, casts)
        if l == 0:
            w_pa, w_pb, w_o = (c.reshape(DEPTH, -1, d) for c in p["casts"])

        attn = _attention(p["qt"], [(p["k"], p["vt"]), (pc["k"], pc["vt"])], batch, 256, 256)
        fm = _fourier(p["rest"], batch)
        merged = _merge(p["rest"], attn, fm, xs, mods, l, lat_row(tm_merge), w_pa, w_pb, w_o,
                        norm_g3, fg, tm_merge)
        if last:
            xs = merged
        else:
            xs, h = merged
            attn_c = _attention(pc["qt"], [(pc["k"], pc["vt"])], batch, n_ctx, n_ctx)
            fm_c = _fourier(pc["rest"], batch)
            cs, hc = _merge(pc["rest"], attn_c, fm_c, cs, mods, l, ctx_mod_row, w_pa, w_pb, w_o,
                            norm_g3, fg, tm_merge)
    return xs.reshape(batch, n_lat, d)
```

```python
import functools
import math

import numpy as np
import jax
import jax.numpy as jnp
from jax import lax
from jax.experimental import pallas as pl
from jax.experimental.pallas import tpu as pltpu

F32 = jnp.float32
BF16 = jnp.bfloat16

D_MODEL = 2048
DEPTH = 2
CTX_LEN = 256
GRID_W = 64
HEAD_DIM = 128
ATT_W = (3 * D_MODEL) // 4
N_Q_HEADS = ATT_W // HEAD_DIM
N_KV_HEADS = 4
Q_PER_KV = N_Q_HEADS // N_KV_HEADS
KV_W = N_KV_HEADS * HEAD_DIM
FNET_W = D_MODEL // 4
N_FNET_GROUPS = 4
FNET_GROUP = FNET_W // N_FNET_GROUPS
ROPE_THETA = 10000.0
ROPE_NFREQ = HEAD_DIM // 4
EPS = 1e-6
IN_W = 2 * ATT_W + 2 * KV_W + 2 * FNET_W + 2 * D_MODEL
Q_SCALE = HEAD_DIM ** -0.5 * math.log2(math.e)

V7X_VMEM_BYTES = 64 * 1024 * 1024
V7X_VMEM_RESERVE_BYTES = 8 * 1024 * 1024
VMEM_LIMIT = V7X_VMEM_BYTES - V7X_VMEM_RESERVE_BYTES

COL_TILE = 512
N_COL_TILES = IN_W // COL_TILE
K_TILE = ATT_W // COL_TILE
V_TILE = K_TILE + 1
ZB_TILE = (2 * ATT_W + 2 * KV_W + FNET_W) // COL_TILE
REST_W = IN_W - ATT_W - 2 * KV_W
REST_ZA_BLOCK = 0
REST_UB_BLOCK = ATT_W // FNET_W
REST_GA_BLOCK = 1
REST_GB_BLOCK = 2
REST_ZB_BLOCK = REST_W // FNET_W - 1
MOD_ROWS = 8


def _silu(x):
    return x * jax.nn.sigmoid(x)


def _ada_kernel(cv_ref, w_ref, b_ref, o_ref):
    s = _silu(cv_ref[...]).astype(BF16)
    o_ref[0] = jnp.dot(s, w_ref[0].astype(BF16), preferred_element_type=F32) + b_ref[0]


def _ada(cvec, w_ada, b_ada):
    tn = 1024
    n = 3 * D_MODEL
    return pl.pallas_call(
        _ada_kernel,
        grid=(DEPTH, n // tn),
        in_specs=[
            pl.BlockSpec((MOD_ROWS, D_MODEL), lambda l, j: (0, 0)),
            pl.BlockSpec((1, D_MODEL, tn), lambda l, j: (l, 0, j)),
            pl.BlockSpec((1, 1, tn), lambda l, j: (l, 0, j)),
        ],
        out_specs=pl.BlockSpec((1, MOD_ROWS, tn), lambda l, j: (l, 0, j)),
        out_shape=jax.ShapeDtypeStruct((DEPTH, MOD_ROWS, n), F32),
        compiler_params=pltpu.CompilerParams(
            dimension_semantics=("arbitrary", "arbitrary"), vmem_limit_bytes=VMEM_LIMIT),
        name="ada",
    )(cvec, w_ada, b_ada.reshape(DEPTH, 1, n))


def _weight_tile(step):
    return jnp.where(step < ZB_TILE, step, jnp.where(step < N_COL_TILES - 1, step + 1, ZB_TILE))


def _head_norm_rope(acc, head_mean, g, c, s1, s2):
    ms = jnp.dot((acc * acc).astype(BF16), head_mean, preferred_element_type=F32)
    y = acc * lax.rsqrt(ms + EPS) * g
    outs = []
    for hh in range(COL_TILE // HEAD_DIM):
        yh = y[:, hh * HEAD_DIM:(hh + 1) * HEAD_DIM]
        outs.append(yh * c + pltpu.roll(yh, HEAD_DIM - ROPE_NFREQ, 1) * s1
                    + pltpu.roll(yh, ROPE_NFREQ, 1) * s2)
    return jnp.concatenate(outs, axis=-1)


def _modnorm(xf, g, scale, shift):
    ms = jnp.mean(xf * xf, axis=-1, keepdims=True)
    return xf * lax.rsqrt(ms + EPS) * g * (1.0 + scale) + shift


def _modnorm_kernel(x_ref, shift_ref, scale_ref, g_ref, h_ref):
    h_ref[...] = _modnorm(x_ref[...], g_ref[0], scale_ref[0], shift_ref[0]).astype(BF16)


def _mod_specs(layer, mod_row_of_tile, parts):
    return [pl.BlockSpec((1, 1, D_MODEL),
                         lambda i, p=p: ((layer * MOD_ROWS + mod_row_of_tile(i)) * 3 + p, 0, 0))
            for p in parts]


MOD_SHIFT, MOD_SCALE, MOD_GATE = 0, 1, 2


def _modnorm_call(x2d, mods, layer, mod_row_of_tile, norm_g, tm):
    t = x2d.shape[0]
    return pl.pallas_call(
        _modnorm_kernel,
        grid=(t // tm,),
        in_specs=[pl.BlockSpec((tm, D_MODEL), lambda i: (i, 0))]
        + _mod_specs(layer, mod_row_of_tile, (MOD_SHIFT, MOD_SCALE))
        + [pl.BlockSpec((1, 1, D_MODEL), lambda i: (layer, 0, 0))],
        out_specs=pl.BlockSpec((tm, D_MODEL), lambda i: (i, 0)),
        out_shape=jax.ShapeDtypeStruct((t, D_MODEL), BF16),
        compiler_params=pltpu.CompilerParams(
            dimension_semantics=("arbitrary",), vmem_limit_bytes=VMEM_LIMIT),
        name="modnorm",
    )(x2d, mods, mods, norm_g)


def _pack_bf16_rows(x):
    return pltpu.bitcast(x.astype(BF16), jnp.uint32)


def _in_proj_kernel(step_lo, names, sub, n_casts, h_ref, w_ref, qg_ref, kg_ref, hm_ref,
                    c_ref, s1_ref, s2_ref, *rest):
    cast_in, out_refs = rest[:n_casts], rest[n_casts:len(rest) - n_casts]
    cast_out = rest[len(rest) - n_casts:]
    out = dict(zip(names, out_refs))
    step = pl.program_id(1) + step_lo
    row_tiles = [slice(r, r + sub) for r in range(0, h_ref.shape[0], sub)]

    def tiles():
        for src, dst in zip(cast_in, cast_out):
            dst[...] = _pack_bf16_rows(src[...])
        w = w_ref[0].astype(BF16)
        for rows in row_tiles:
            yield rows, jnp.dot(h_ref[rows, :], w, preferred_element_type=F32)

    def head_tiles(g_ref):
        for rows, acc in tiles():
            yield rows, _head_norm_rope(acc, hm_ref[...], g_ref[0], c_ref[rows, :],
                                        s1_ref[rows, :], s2_ref[rows, :])

    if "qt" in out:
        @pl.when(step < K_TILE)
        def _():
            for rows, qh in head_tiles(qg_ref):
                out["qt"][:, rows] = qh.T.astype(BF16)

    if "k" in out:
        @pl.when(step == K_TILE)
        def _():
            for rows, kh in head_tiles(kg_ref):
                out["k"][rows, :] = kh.astype(BF16)

    if "vt" in out:
        @pl.when(step == V_TILE)
        def _():
            for rows, acc in tiles():
                out["vt"][:, rows] = acc.T.astype(BF16)

    if "rest" in out:
        @pl.when(step > V_TILE)
        def _():
            for rows, acc in tiles():
                out["rest"][rows, :] = acc.astype(BF16)


CAST_STEPS = 64


def _in_proj(h, w_in, layer, q_g, k_g, tables, step_lo, step_hi, group, sub, casts=()):
    t = h.shape[0]
    n_steps = step_hi - step_lo
    assert not casts or (t // group) * n_steps >= CAST_STEPS

    def cast_spec(rows):
        return pl.BlockSpec((rows // CAST_STEPS, D_MODEL),
                            lambda i, j: (jnp.minimum(i * n_steps + j, CAST_STEPS - 1), 0))

    cast_in_specs = [cast_spec(a.shape[0]) for a in casts]
    cast_out_specs = [cast_spec(a.shape[0] // 2) for a in casts]
    cast_shapes = [jax.ShapeDtypeStruct((a.shape[0] // 2, a.shape[1]), jnp.uint32)
                   for a in casts]
    c_tab, s1_tab, s2_tab = tables
    n_rest = REST_W // COL_TILE
    heads_per_tile = COL_TILE // HEAD_DIM
    head_mean = jnp.asarray(
        np.kron(np.eye(heads_per_tile), np.full((HEAD_DIM, HEAD_DIM), 1.0 / HEAD_DIM)), BF16)
    specs = {
        "qt": (pl.BlockSpec((COL_TILE, group),
                            lambda i, j: (jnp.clip(j + step_lo, 0, K_TILE - 1), i)),
               jax.ShapeDtypeStruct((ATT_W, t), BF16), step_lo < K_TILE),
        "k": (pl.BlockSpec((group, COL_TILE), lambda i, j: (i, 0)),
              jax.ShapeDtypeStruct((t, KV_W), BF16), step_lo <= K_TILE < step_hi),
        "vt": (pl.BlockSpec((COL_TILE, group), lambda i, j: (0, i)),
               jax.ShapeDtypeStruct((KV_W, t), BF16), step_lo <= V_TILE < step_hi),
        "rest": (pl.BlockSpec((group, COL_TILE),
                              lambda i, j: (i, jnp.clip(j + step_lo - V_TILE - 1, 0, n_rest - 1))),
                 jax.ShapeDtypeStruct((t, REST_W), BF16), step_hi > V_TILE + 1),
    }
    names = tuple(n for n, (_, _, present) in specs.items() if present)
    tab_spec = pl.BlockSpec((group, HEAD_DIM), lambda i, j: (0, 0))
    gain_spec = pl.BlockSpec((1, 1, COL_TILE), lambda i, j: (layer, 0, 0))
    outs = pl.pallas_call(
        functools.partial(_in_proj_kernel, step_lo, names, sub, len(casts)),
        grid=(t // group, n_steps),
        in_specs=[
            pl.BlockSpec((group, D_MODEL), lambda i, j: (i, 0)),
            pl.BlockSpec((1, D_MODEL, COL_TILE),
                         lambda i, j: (layer, 0, _weight_tile(j + step_lo))),
            gain_spec, gain_spec,
            pl.BlockSpec((COL_TILE, COL_TILE), lambda i, j: (0, 0)),
            tab_spec, tab_spec, tab_spec,
        ] + cast_in_specs,
        out_specs=[specs[n][0] for n in names] + cast_out_specs,
        out_shape=[specs[n][1] for n in names] + cast_shapes,
        compiler_params=pltpu.CompilerParams(
            dimension_semantics=("arbitrary", "arbitrary"), vmem_limit_bytes=VMEM_LIMIT),
        name="in_proj",
    )(h, w_in, q_g, k_g, head_mean, c_tab, s1_tab, s2_tab, *casts)
    result = dict(zip(names, outs))
    if casts:
        result["casts"] = outs[len(names):]
    return result


ONES_ROWS = 16


def _col_reduce8(x, op):
    rows, w = x.shape
    return op(x.reshape(rows // 8, 8, w), axis=0)


def _attn_kernel(chunks, tq, q_ref, *rest):
    n_sets = 1 + max(c[0] for c in chunks)
    k_refs, vt_refs = rest[:n_sets], rest[n_sets:2 * n_sets]
    o_ref, s_scr, m_scr = rest[2 * n_sets:]
    width = Q_PER_KV * tq

    @pl.when(pl.program_id(0) == 0)
    def _():
        s_scr[...] = jnp.zeros_like(s_scr)
        m_scr[...] = jnp.zeros_like(m_scr)

    q_t = jnp.concatenate(
        [q_ref[i * HEAD_DIM:(i + 1) * HEAD_DIM, :] for i in range(Q_PER_KV)], axis=1)
    m_prev = m_scr[...]
    m_run = jnp.full((8, width), -jnp.inf, F32)
    acc = jnp.zeros((HEAD_DIM + ONES_ROWS, width), F32)
    for t, r0, rows, s0 in chunks:
        p = jnp.exp2(s_scr[s0:s0 + rows, :] - m_prev).astype(BF16)
        vt_ones = jnp.concatenate(
            [vt_refs[t][:, r0:r0 + rows], jnp.ones((ONES_ROWS, rows), BF16)], axis=0)
        acc = acc + jnp.dot(vt_ones, p, preferred_element_type=F32)
        s_new = jnp.dot(k_refs[t][r0:r0 + rows, :], q_t, preferred_element_type=F32)
        s_scr[s0:s0 + rows, :] = s_new
        m_run = jnp.maximum(m_run, _col_reduce8(s_new, jnp.max))
    m_scr[...] = m_run.max(axis=0, keepdims=True)
    o_t = acc[:HEAD_DIM] / acc[HEAD_DIM:HEAD_DIM + 1]
    for i in range(Q_PER_KV):
        o_ref[:, i * HEAD_DIM:(i + 1) * HEAD_DIM] = o_t[:, i * tq:(i + 1) * tq].T.astype(BF16)


def _attention(qt, kv_sets, batch, tq, key_chunk):
    t = qt.shape[1]
    nq = t // batch // tq
    n_tiles = batch * N_KV_HEADS * nq
    gw = Q_PER_KV * HEAD_DIM

    def tile(n):
        return n // (N_KV_HEADS * nq), (n // nq) % N_KV_HEADS, n % nq

    def score_tile(n):
        return tile(jnp.minimum(n, n_tiles - 1))

    def out_tile(n):
        return tile(jnp.maximum(n - 1, 0))

    def q_map(n):
        b, g, i = score_tile(n)
        return g, b * nq + i

    def o_map(n):
        b, g, i = out_tile(n)
        return b * nq + i, g

    def k_map(n):
        b, g, _ = score_tile(n)
        return b, g

    def vt_map(n):
        b, g, _ = out_tile(n)
        return g, b

    chunks, k_specs, vt_specs, s_rows = [], [], [], 0
    for s, (k, _) in enumerate(kv_sets):
        m_t = k.shape[0] // batch
        step = min(key_chunk, m_t)
        for r0 in range(0, m_t, step):
            chunks.append((s, r0, step, s_rows + r0))
        s_rows += m_t
        k_specs.append(pl.BlockSpec((m_t, HEAD_DIM), k_map))
        vt_specs.append(pl.BlockSpec((HEAD_DIM, m_t), vt_map))
    return pl.pallas_call(
        functools.partial(_attn_kernel, tuple(chunks), tq),
        grid=(n_tiles + 1,),
        in_specs=[pl.BlockSpec((gw, tq), q_map)] + k_specs + vt_specs,
        out_specs=pl.BlockSpec((tq, gw), o_map),
        out_shape=jax.ShapeDtypeStruct((t, ATT_W), BF16),
        scratch_shapes=[pltpu.VMEM((s_rows, Q_PER_KV * tq), F32),
                        pltpu.VMEM((1, Q_PER_KV * tq), F32)],
        compiler_params=pltpu.CompilerParams(
            dimension_semantics=("arbitrary",), vmem_limit_bytes=VMEM_LIMIT),
        name="attn",
    )(qt, *[k for k, _ in kv_sets], *[vt for _, vt in kv_sets])


DFT_PAD_ROWS = 16
FLIP_BLOCK = 256


def _dft_tables(n):
    h = n // 2
    k = np.arange(h + 1, dtype=np.int64)[:, None]
    j = np.arange(h, dtype=np.int64)[None, :]
    ang = 2.0 * np.pi * ((k * j) % n).astype(np.float64) / n
    c_half = np.zeros((h + DFT_PAD_ROWS, h))
    c_half[:h + 1] = np.cos(ang) / np.sqrt(n)
    s_half = np.sin(ang[:h]) / np.sqrt(n)
    blk = min(FLIP_BLOCK, h)
    flip = np.zeros((blk, blk))
    flip[np.arange(1, blk), blk - np.arange(1, blk)] = 1.0
    ch = np.arange(FNET_GROUP, dtype=np.int64)
    ang_c = 2.0 * np.pi * ((ch[:, None] * ch[None, :]) % FNET_GROUP) / FNET_GROUP
    cc, sc = np.cos(ang_c) / np.sqrt(FNET_GROUP), np.sin(ang_c) / np.sqrt(FNET_GROUP)
    return tuple(a.astype(np.float32) for a in (c_half, s_half, flip, cc, sc))


def _flip_rows(x, flip):
    h, w = x.shape
    blk = flip.shape[0]
    nb = h // blk
    is_row0 = lax.broadcasted_iota(jnp.int32, (blk, w), 0) == 0
    blocks = [x[b * blk:(b + 1) * blk, :] for b in range(nb)]
    out = []
    for b in range(nb):
        inner = jnp.dot(flip, blocks[nb - b - 1], preferred_element_type=F32)
        first = blocks[(nb - b) % nb][0:1, :].astype(F32)
        out.append(jnp.where(is_row0, first, inner))
    return jnp.concatenate(out, axis=0)


def _fourier_kernel(n, u_ref, ch_ref, sh_ref, flip_ref, cc_ref, sc_ref, o_ref, a_scr, b_scr):
    h = n // 2
    is_row0 = lax.broadcasted_iota(jnp.int32, (h, FNET_W), 0) == 0
    u_top = u_ref[0:h, :].astype(F32)
    v = _flip_rows(u_ref[h:n, :], flip_ref[...])
    even = (u_top + jnp.where(is_row0, 0.0, v)).astype(BF16)
    odd = jnp.where(is_row0, 0.0, u_top - v).astype(BF16)
    pad_row0 = lax.broadcasted_iota(jnp.int32, (DFT_PAD_ROWS, FNET_W), 0) == 0
    mid = jnp.where(pad_row0, v[0:1, :], 0.0).astype(BF16)
    a_mid = []
    for g in range(N_FNET_GROUPS):
        cols = slice(g * FNET_GROUP, (g + 1) * FNET_GROUP)
        a_scr[:, cols] = jnp.dot(even[:, cols], cc_ref[...],
                                 preferred_element_type=F32).astype(BF16)
        b_scr[:, cols] = jnp.dot(odd[:, cols], sc_ref[...],
                                 preferred_element_type=F32).astype(BF16)
        a_mid.append(jnp.dot(mid[:, cols], cc_ref[...], preferred_element_type=F32)[0:1, :])
    a_mid = jnp.concatenate(a_mid, axis=1) * (n ** -0.5)
    p = jnp.dot(ch_ref[...], a_scr[...], preferred_element_type=F32)
    k_is_even = (lax.broadcasted_iota(jnp.int32, p.shape, 0) & 1) == 0
    p = p + jnp.where(k_is_even, a_mid, -a_mid)
    q = jnp.dot(sh_ref[...], b_scr[...], preferred_element_type=F32)
    o_ref[0:h, :] = (p[:h] - q).astype(BF16)
    z = jnp.where(is_row0, p[h:h + 1], p[:h] + q).astype(BF16)
    o_ref[h:n, :] = _flip_rows(z, flip_ref[...]).astype(BF16)


def _fourier(rest, batch):
    t = rest.shape[0]
    n = t // batch
    tables = [jnp.asarray(a).astype(BF16) for a in _dft_tables(n)]

    def whole(a):
        return pl.BlockSpec(a.shape, lambda b: (0, 0), pipeline_mode=pl.Buffered(1))

    return pl.pallas_call(
        functools.partial(_fourier_kernel, n),
        grid=(batch,),
        in_specs=[pl.BlockSpec((n, FNET_W), lambda b: (b, REST_UB_BLOCK))]
        + [whole(a) for a in tables],
        out_specs=pl.BlockSpec((n, FNET_W), lambda b: (b, 0)),
        out_shape=jax.ShapeDtypeStruct((t, FNET_W), BF16),
        scratch_shapes=[pltpu.VMEM((n // 2, FNET_W), BF16), pltpu.VMEM((n // 2, FNET_W), BF16)],
        compiler_params=pltpu.CompilerParams(
            dimension_semantics=("arbitrary",), vmem_limit_bytes=VMEM_LIMIT),
        name="fourier",
    )(rest, *tables)


def _merge_kernel(final, attn_ref, za_ref, fm_ref, zb_ref, ga_ref, gb_ref, x_ref, gate_ref,
                  wpa_ref, wpb_ref, wo_ref, *rest):
    def weight(w_ref):
        return pltpu.bitcast(w_ref[0], BF16)

    ta = (attn_ref[...].astype(F32) * _silu(za_ref[...].astype(F32))).astype(BF16)
    ya = jnp.dot(ta, weight(wpa_ref), preferred_element_type=F32)
    tb = (fm_ref[...].astype(F32) * _silu(zb_ref[...].astype(F32))).astype(BF16)
    yb = jnp.dot(tb, weight(wpb_ref), preferred_element_type=F32)
    mix = (jax.nn.sigmoid(ga_ref[...].astype(F32)) * ya
           + jax.nn.sigmoid(gb_ref[...].astype(F32)) * yb).astype(BF16)
    y = jnp.dot(mix, weight(wo_ref), preferred_element_type=F32)
    out = x_ref[...] + gate_ref[0] * y
    if final:
        fg_ref, o_ref = rest
        ms = jnp.mean(out * out, axis=-1, keepdims=True)
        o_ref[...] = out * lax.rsqrt(ms + EPS) * fg_ref[...]
    else:
        shift_ref, scale_ref, g_ref, o_ref, h_ref = rest
        o_ref[...] = out
        h_ref[...] = _modnorm(out, g_ref[0], scale_ref[0], shift_ref[0]).astype(BF16)


def _merge(rest, attn, fm, x2d, mods, layer, mod_row_of_tile, w_pa, w_pb, w_o, norm_g, final_g,
           tm):
    t = x2d.shape[0]
    final = layer == DEPTH - 1

    def rows(width, block=0):
        return pl.BlockSpec((tm, width), lambda i: (i, block))

    def weight(k):
        return pl.BlockSpec((1, k // 2, D_MODEL), lambda i: (layer, 0, 0),
                            pipeline_mode=pl.Buffered(1))

    in_specs = [
        rows(ATT_W), rows(ATT_W, REST_ZA_BLOCK), rows(FNET_W), rows(FNET_W, REST_ZB_BLOCK),
        rows(D_MODEL, REST_GA_BLOCK), rows(D_MODEL, REST_GB_BLOCK),
        rows(D_MODEL),
    ] + _mod_specs(layer, mod_row_of_tile, (MOD_GATE,)) + [
        weight(ATT_W), weight(FNET_W), weight(D_MODEL)]
    args = [attn, rest, fm, rest, rest, rest, x2d, mods, w_pa, w_pb, w_o]
    stream = jax.ShapeDtypeStruct((t, D_MODEL), F32)
    if final:
        in_specs += [pl.BlockSpec((1, D_MODEL), lambda i: (0, 0))]
        args += [final_g]
        out_specs, out_shape = rows(D_MODEL), stream
    else:
        in_specs += _mod_specs(layer + 1, mod_row_of_tile, (MOD_SHIFT, MOD_SCALE))
        in_specs += [pl.BlockSpec((1, 1, D_MODEL), lambda i: (layer + 1, 0, 0))]
        args += [mods, mods, norm_g]
        out_specs = [rows(D_MODEL), rows(D_MODEL)]
        out_shape = [stream, jax.ShapeDtypeStruct((t, D_MODEL), BF16)]
    return pl.pallas_call(
        functools.partial(_merge_kernel, final),
        grid=(t // tm,),
        in_specs=in_specs,
        out_specs=out_specs,
        out_shape=out_shape,
        compiler_params=pltpu.CompilerParams(
            dimension_semantics=("arbitrary",), vmem_limit_bytes=VMEM_LIMIT),
        name="merge",
    )(*args)


def _rope_tables(n_tokens):
    t = np.arange(n_tokens)
    pos = np.stack([t // GRID_W, t % GRID_W], axis=1).astype(np.float32)
    inv = (np.float32(ROPE_THETA) ** (-np.arange(ROPE_NFREQ, dtype=np.float32) / ROPE_NFREQ))
    ang = (pos[:, :, None] * inv[None, None, :]).astype(np.float32).astype(np.float64)
    cos, sin = np.cos(ang), np.sin(ang)
    zero = np.zeros_like(sin)
    c = np.stack([cos, cos], axis=2).reshape(n_tokens, HEAD_DIM)
    s1 = np.stack([-sin, zero], axis=2).reshape(n_tokens, HEAD_DIM)
    s2 = np.stack([zero, sin], axis=2).reshape(n_tokens, HEAD_DIM)
    return tuple(jnp.asarray(a.astype(np.float32)) for a in (c, s1, s2))


def _identity_tables(n_tokens):
    one = jnp.ones((n_tokens, HEAD_DIM), F32)
    zero = jnp.zeros((n_tokens, HEAD_DIM), F32)
    return one, zero, zero


def kernel(x, c, ctx, c_ctx, w_ada, b_ada, norm_g, w_in, q_norm_g, k_norm_g,
           w_proj_a, w_proj_b, w_out, final_g):
    batch, n_lat, d = x.shape
    n_ctx = ctx.shape[1]
    tm_norm, tm_merge, sub = 1024, 256, 512
    ctx_row = batch

    cvec = jnp.concatenate(
        [c, c_ctx[None], jnp.zeros((MOD_ROWS - batch - 1, d), F32)], axis=0)
    mods = _ada(cvec, w_ada, b_ada).reshape(DEPTH * MOD_ROWS * 3, 1, d)

    rope = _rope_tables(n_lat)
    no_rope = _identity_tables(batch * n_ctx)
    heads_per_tile = COL_TILE // HEAD_DIM
    qg = jnp.tile(q_norm_g * Q_SCALE, (1, heads_per_tile)).reshape(DEPTH, 1, COL_TILE)
    kg = jnp.tile(k_norm_g, (1, heads_per_tile)).reshape(DEPTH, 1, COL_TILE)
    merge_w = (w_proj_a, w_proj_b, w_out)
    norm_g3 = norm_g.reshape(DEPTH, 1, d)
    fg = final_g.reshape(1, d)

    def lat_row(tm):
        return lambda i: i // (n_lat // tm)

    def ctx_mod_row(i):
        return ctx_row

    xs = x.reshape(batch * n_lat, d)
    cs = ctx.reshape(batch * n_ctx, d)
    h = _modnorm_call(xs, mods, 0, lat_row(tm_norm), norm_g3, tm_norm)
    hc = _modnorm_call(cs, mods, 0, ctx_mod_row, norm_g3, tm_norm // 2)
    for l in range(DEPTH):
        last = l == DEPTH - 1
        lo, hi = (K_TILE, V_TILE + 1) if last else (0, N_COL_TILES)
        pc = _in_proj(hc, w_in, l, qg, kg, no_rope, lo, hi, batch * n_ctx, sub)
        casts = tuple(w.reshape(-1, d) for w in merge_w) if l == 0 else ()
        p = _in_proj(h, w_in, l, qg, kg, rope, 0, N_COL_TILES, n_lat, sub, casts)
        if l == 0:
            w_pa, w_pb, w_o = (c.reshape(DEPTH, -1, d) for c in p["casts"])

        attn = _attention(p["qt"], [(p["k"], p["vt"]), (pc["k"], pc["vt"])], batch, 256, 256)
        fm = _fourier(p["rest"], batch)
        merged = _merge(p["rest"], attn, fm, xs, mods, l, lat_row(tm_merge), w_pa, w_pb, w_o,
                        norm_g3, fg, tm_merge)
        if last:
            xs = merged
        else:
            xs, h = merged
            attn_c = _attention(pc["qt"], [(pc["k"], pc["vt"])], batch, n_ctx, n_ctx)
            fm_c = _fourier(pc["rest"], batch)
            cs, hc = _merge(pc["rest"], attn_c, fm_c, cs, mods, l, ctx_mod_row, w_pa, w_pb, w_o,
                            norm_g3, fg, tm_merge)
    return xs.reshape(batch, n_lat, d)
```

```python
import functools
import math

import numpy as np
import jax
import jax.numpy as jnp
from jax import lax
from jax.experimental import pallas as pl
from jax.experimental.pallas import tpu as pltpu

F32 = jnp.float32
BF16 = jnp.bfloat16

D_MODEL = 2048
DEPTH = 2
GRID_W = 64
HEAD_DIM = 128
ATT_W = (3 * D_MODEL) // 4
N_Q_HEADS = ATT_W // HEAD_DIM
N_KV_HEADS = 4
Q_PER_KV = N_Q_HEADS // N_KV_HEADS
KV_W = N_KV_HEADS * HEAD_DIM
FNET_W = D_MODEL // 4
N_FNET_GROUPS = 4
FNET_GROUP = FNET_W // N_FNET_GROUPS
ROPE_THETA = 10000.0
ROPE_NFREQ = HEAD_DIM // 4
EPS = 1e-6
IN_W = 2 * ATT_W + 2 * KV_W + 2 * FNET_W + 2 * D_MODEL
Q_SCALE = HEAD_DIM ** -0.5 * math.log2(math.e)

V7X_VMEM_BYTES = 64 * 1024 * 1024
V7X_VMEM_RESERVE_BYTES = 8 * 1024 * 1024
VMEM_LIMIT = V7X_VMEM_BYTES - V7X_VMEM_RESERVE_BYTES

COL_TILE = 512
N_COL_TILES = IN_W // COL_TILE
K_TILE = ATT_W // COL_TILE
V_TILE = K_TILE + 1
ZB_TILE = (2 * ATT_W + 2 * KV_W + FNET_W) // COL_TILE
REST_W = IN_W - ATT_W - 2 * KV_W
REST_ZA_BLOCK = 0
REST_UB_BLOCK = ATT_W // FNET_W
REST_GA_BLOCK = 1
REST_GB_BLOCK = 2
REST_ZB_BLOCK = REST_W // FNET_W - 1
MOD_ROWS = 8


def _silu(x):
    return x * jax.nn.sigmoid(x)


def _ada_kernel(cv_ref, w_ref, b_ref, o_ref):
    s = _silu(cv_ref[...]).astype(BF16)
    o_ref[0] = jnp.dot(s, w_ref[0].astype(BF16), preferred_element_type=F32) + b_ref[0]


def _ada(cvec, w_ada, b_ada):
    tn = 1024
    n = 3 * D_MODEL
    return pl.pallas_call(
        _ada_kernel,
        grid=(DEPTH, n // tn),
        in_specs=[
            pl.BlockSpec((MOD_ROWS, D_MODEL), lambda l, j: (0, 0)),
            pl.BlockSpec((1, D_MODEL, tn), lambda l, j: (l, 0, j)),
            pl.BlockSpec((1, 1, tn), lambda l, j: (l, 0, j)),
        ],
        out_specs=pl.BlockSpec((1, MOD_ROWS, tn), lambda l, j: (l, 0, j)),
        out_shape=jax.ShapeDtypeStruct((DEPTH, MOD_ROWS, n), F32),
        compiler_params=pltpu.CompilerParams(
            dimension_semantics=("arbitrary", "arbitrary"), vmem_limit_bytes=VMEM_LIMIT),
        name="ada",
    )(cvec, w_ada, b_ada.reshape(DEPTH, 1, n))


def _weight_tile(step):
    return jnp.where(step < ZB_TILE, step, jnp.where(step < N_COL_TILES - 1, step + 1, ZB_TILE))


def _head_norm_rope(acc, head_mean, g, c, s1, s2):
    ms = jnp.dot((acc * acc).astype(BF16), head_mean, preferred_element_type=F32)
    y = acc * lax.rsqrt(ms + EPS) * g
    outs = []
    for hh in range(COL_TILE // HEAD_DIM):
        yh = y[:, hh * HEAD_DIM:(hh + 1) * HEAD_DIM]
        outs.append(yh * c + pltpu.roll(yh, HEAD_DIM - ROPE_NFREQ, 1) * s1
                    + pltpu.roll(yh, ROPE_NFREQ, 1) * s2)
    return jnp.concatenate(outs, axis=-1)


def _modnorm(xf, g, scale, shift):
    ms = jnp.mean(xf * xf, axis=-1, keepdims=True)
    return xf * lax.rsqrt(ms + EPS) * g * (1.0 + scale) + shift


NORM_CHUNK = 16


def _modnorm_kernel(x_ref, shift_ref, scale_ref, g_ref, h_ref):
    gain = g_ref[0] * (1.0 + scale_ref[0])
    shift = shift_ref[0]

    def chunk(c, carry):
        rows = pl.ds(pl.multiple_of(c * NORM_CHUNK, NORM_CHUNK), NORM_CHUNK)
        xf = x_ref[rows, :]
        ms = jnp.mean(xf * xf, axis=-1, keepdims=True)
        h_ref[rows, :] = (xf * lax.rsqrt(ms + EPS) * gain + shift).astype(BF16)
        return carry

    lax.fori_loop(0, x_ref.shape[0] // NORM_CHUNK, chunk, None, unroll=8)


def _mod_specs(layer, mod_row_of_tile, parts):
    return [pl.BlockSpec((1, 1, D_MODEL),
                         lambda i, p=p: ((layer * MOD_ROWS + mod_row_of_tile(i)) * 3 + p, 0, 0))
            for p in parts]


MOD_SHIFT, MOD_SCALE, MOD_GATE = 0, 1, 2


def _modnorm_call(x2d, mods, layer, mod_row_of_tile, norm_g, tm):
    t = x2d.shape[0]
    return pl.pallas_call(
        _modnorm_kernel,
        grid=(t // tm,),
        in_specs=[pl.BlockSpec((tm, D_MODEL), lambda i: (i, 0))]
        + _mod_specs(layer, mod_row_of_tile, (MOD_SHIFT, MOD_SCALE))
        + [pl.BlockSpec((1, 1, D_MODEL), lambda i: (layer, 0, 0))],
        out_specs=pl.BlockSpec((tm, D_MODEL), lambda i: (i, 0)),
        out_shape=jax.ShapeDtypeStruct((t, D_MODEL), BF16),
        compiler_params=pltpu.CompilerParams(
            dimension_semantics=("arbitrary",), vmem_limit_bytes=VMEM_LIMIT),
        name="modnorm",
    )(x2d, mods, mods, norm_g)


def _pack_bf16_rows(x):
    return pltpu.bitcast(x.astype(BF16), jnp.uint32)


def _in_proj_kernel(step_lo, names, sub, n_casts, h_ref, w_ref, qg_ref, kg_ref, hm_ref,
                    c_ref, s1_ref, s2_ref, *rest):
    cast_in, out_refs = rest[:n_casts], rest[n_casts:len(rest) - n_casts]
    cast_out = rest[len(rest) - n_casts:]
    out = dict(zip(names, out_refs))
    step = pl.program_id(1) + step_lo
    row_tiles = [slice(r, r + sub) for r in range(0, h_ref.shape[0], sub)]

    def tiles():
        for src, dst in zip(cast_in, cast_out):
            dst[...] = _pack_bf16_rows(src[...])
        w = w_ref[0].astype(BF16)
        for rows in row_tiles:
            yield rows, jnp.dot(h_ref[rows, :], w, preferred_element_type=F32)

    def head_tiles(g_ref):
        for rows, acc in tiles():
            yield rows, _head_norm_rope(acc, hm_ref[...], g_ref[0], c_ref[rows, :],
                                        s1_ref[rows, :], s2_ref[rows, :])

    if "qt" in out:
        @pl.when(step < K_TILE)
        def _():
            for rows, qh in head_tiles(qg_ref):
                out["qt"][:, rows] = qh.T.astype(BF16)

    if "k" in out:
        @pl.when(step == K_TILE)
        def _():
            for rows, kh in head_tiles(kg_ref):
                out["k"][rows, :] = kh.astype(BF16)

    if "vt" in out:
        @pl.when(step == V_TILE)
        def _():
            for rows, acc in tiles():
                out["vt"][:, rows] = acc.T.astype(BF16)

    if "rest" in out:
        @pl.when(step > V_TILE)
        def _():
            for rows, acc in tiles():
                out["rest"][rows, :] = acc.astype(BF16)


CAST_STEPS = 64


def _in_proj(h, w_in, layer, q_g, k_g, tables, step_lo, step_hi, group, sub, casts=()):
    t = h.shape[0]
    n_steps = step_hi - step_lo
    assert not casts or (t // group) * n_steps >= CAST_STEPS

    def cast_spec(rows):
        return pl.BlockSpec((rows // CAST_STEPS, D_MODEL),
                            lambda i, j: (jnp.minimum(i * n_steps + j, CAST_STEPS - 1), 0))

    cast_in_specs = [cast_spec(a.shape[0]) for a in casts]
    cast_out_specs = [cast_spec(a.shape[0] // 2) for a in casts]
    cast_shapes = [jax.ShapeDtypeStruct((a.shape[0] // 2, a.shape[1]), jnp.uint32)
                   for a in casts]
    c_tab, s1_tab, s2_tab = tables
    n_rest = REST_W // COL_TILE
    heads_per_tile = COL_TILE // HEAD_DIM
    head_mean = jnp.asarray(
        np.kron(np.eye(heads_per_tile), np.full((HEAD_DIM, HEAD_DIM), 1.0 / HEAD_DIM)), BF16)
    specs = {
        "qt": (pl.BlockSpec((COL_TILE, group),
                            lambda i, j: (jnp.clip(j + step_lo, 0, K_TILE - 1), i)),
               jax.ShapeDtypeStruct((ATT_W, t), BF16), step_lo < K_TILE),
        "k": (pl.BlockSpec((group, COL_TILE), lambda i, j: (i, 0)),
              jax.ShapeDtypeStruct((t, KV_W), BF16), step_lo <= K_TILE < step_hi),
        "vt": (pl.BlockSpec((COL_TILE, group), lambda i, j: (0, i)),
               jax.ShapeDtypeStruct((KV_W, t), BF16), step_lo <= V_TILE < step_hi),
        "rest": (pl.BlockSpec((group, COL_TILE),
                              lambda i, j: (i, jnp.clip(j + step_lo - V_TILE - 1, 0, n_rest - 1))),
                 jax.ShapeDtypeStruct((t, REST_W), BF16), step_hi > V_TILE + 1),
    }
    names = tuple(n for n, (_, _, present) in specs.items() if present)
    tab_spec = pl.BlockSpec((group, HEAD_DIM), lambda i, j: (0, 0))
    gain_spec = pl.BlockSpec((1, 1, COL_TILE), lambda i, j: (layer, 0, 0))
    outs = pl.pallas_call(
        functools.partial(_in_proj_kernel, step_lo, names, sub, len(casts)),
        grid=(t // group, n_steps),
        in_specs=[
            pl.BlockSpec((group, D_MODEL), lambda i, j: (i, 0)),
            pl.BlockSpec((1, D_MODEL, COL_TILE),
                         lambda i, j: (layer, 0, _weight_tile(j + step_lo))),
            gain_spec, gain_spec,
            pl.BlockSpec((COL_TILE, COL_TILE), lambda i, j: (0, 0)),
            tab_spec, tab_spec, tab_spec,
        ] + cast_in_specs,
        out_specs=[specs[n][0] for n in names] + cast_out_specs,
        out_shape=[specs[n][1] for n in names] + cast_shapes,
        compiler_params=pltpu.CompilerParams(
            dimension_semantics=("arbitrary", "arbitrary"), vmem_limit_bytes=VMEM_LIMIT),
        name="in_proj",
    )(h, w_in, q_g, k_g, head_mean, c_tab, s1_tab, s2_tab, *casts)
    result = dict(zip(names, outs))
    if casts:
        result["casts"] = outs[len(names):]
    return result


ONES_ROWS = 16


def _col_reduce8(x, op):
    rows, w = x.shape
    return op(x.reshape(rows // 8, 8, w), axis=0)


def _attn_kernel(chunks, tq, q_ref, *rest):
    n_sets = 1 + max(c[0] for c in chunks)
    k_refs, vt_refs = rest[:n_sets], rest[n_sets:2 * n_sets]
    o_ref, s_scr, m_scr = rest[2 * n_sets:]
    width = Q_PER_KV * tq

    @pl.when(pl.program_id(0) == 0)
    def _():
        s_scr[...] = jnp.zeros_like(s_scr)
        m_scr[...] = jnp.zeros_like(m_scr)

    q_t = jnp.concatenate(
        [q_ref[i * HEAD_DIM:(i + 1) * HEAD_DIM, :] for i in range(Q_PER_KV)], axis=1)
    m_prev = m_scr[...]
    m_run = jnp.full((8, width), -jnp.inf, F32)
    acc = jnp.zeros((HEAD_DIM + ONES_ROWS, width), F32)
    for t, r0, rows, s0 in chunks:
        p = jnp.exp2(s_scr[s0:s0 + rows, :] - m_prev).astype(BF16)
        vt_ones = jnp.concatenate(
            [vt_refs[t][:, r0:r0 + rows], jnp.ones((ONES_ROWS, rows), BF16)], axis=0)
        acc = acc + jnp.dot(vt_ones, p, preferred_element_type=F32)
        s_new = jnp.dot(k_refs[t][r0:r0 + rows, :], q_t, preferred_element_type=F32)
        s_scr[s0:s0 + rows, :] = s_new
        m_run = jnp.maximum(m_run, _col_reduce8(s_new, jnp.max))
    m_scr[...] = m_run.max(axis=0, keepdims=True)
    o_t = acc[:HEAD_DIM] / acc[HEAD_DIM:HEAD_DIM + 1]
    for i in range(Q_PER_KV):
        o_ref[:, i * HEAD_DIM:(i + 1) * HEAD_DIM] = o_t[:, i * tq:(i + 1) * tq].T.astype(BF16)


def _attention(qt, kv_sets, batch, tq, key_chunk):
    t = qt.shape[1]
    nq = t // batch // tq
    n_tiles = batch * N_KV_HEADS * nq
    gw = Q_PER_KV * HEAD_DIM

    def tile(n):
        return n // (N_KV_HEADS * nq), (n // nq) % N_KV_HEADS, n % nq

    def score_tile(n):
        return tile(jnp.minimum(n, n_tiles - 1))

    def out_tile(n):
        return tile(jnp.maximum(n - 1, 0))

    def q_map(n):
        b, g, i = score_tile(n)
        return g, b * nq + i

    def o_map(n):
        b, g, i = out_tile(n)
        return b * nq + i, g

    def k_map(n):
        b, g, _ = score_tile(n)
        return b, g

    def vt_map(n):
        b, g, _ = out_tile(n)
        return g, b

    chunks, k_specs, vt_specs, s_rows = [], [], [], 0
    for s, (k, _) in enumerate(kv_sets):
        m_t = k.shape[0] // batch
        step = min(key_chunk, m_t)
        for r0 in range(0, m_t, step):
            chunks.append((s, r0, step, s_rows + r0))
        s_rows += m_t
        k_specs.append(pl.BlockSpec((m_t, HEAD_DIM), k_map))
        vt_specs.append(pl.BlockSpec((HEAD_DIM, m_t), vt_map))
    return pl.pallas_call(
        functools.partial(_attn_kernel, tuple(chunks), tq),
        grid=(n_tiles + 1,),
        in_specs=[pl.BlockSpec((gw, tq), q_map)] + k_specs + vt_specs,
        out_specs=pl.BlockSpec((tq, gw), o_map),
        out_shape=jax.ShapeDtypeStruct((t, ATT_W), BF16),
        scratch_shapes=[pltpu.VMEM((s_rows, Q_PER_KV * tq), F32),
                        pltpu.VMEM((1, Q_PER_KV * tq), F32)],
        compiler_params=pltpu.CompilerParams(
            dimension_semantics=("arbitrary",), vmem_limit_bytes=VMEM_LIMIT),
        name="attn",
    )(qt, *[k for k, _ in kv_sets], *[vt for _, vt in kv_sets])


DFT_PAD_ROWS = 16
FLIP_BLOCK = 256


def _dft_tables(n):
    h = n // 2
    k = np.arange(h + 1, dtype=np.int64)[:, None]
    j = np.arange(h, dtype=np.int64)[None, :]
    ang = 2.0 * np.pi * ((k * j) % n).astype(np.float64) / n
    c_half = np.zeros((h + DFT_PAD_ROWS, h))
    c_half[:h + 1] = np.cos(ang) / np.sqrt(n)
    s_half = np.sin(ang[:h]) / np.sqrt(n)
    blk = min(FLIP_BLOCK, h)
    flip = np.zeros((blk, blk))
    flip[np.arange(1, blk), blk - np.arange(1, blk)] = 1.0
    ch = np.arange(FNET_GROUP, dtype=np.int64)
    ang_c = 2.0 * np.pi * ((ch[:, None] * ch[None, :]) % FNET_GROUP) / FNET_GROUP
    cc, sc = np.cos(ang_c) / np.sqrt(FNET_GROUP), np.sin(ang_c) / np.sqrt(FNET_GROUP)
    return tuple(a.astype(np.float32) for a in (c_half, s_half, flip, cc, sc))


def _flip_rows(x, flip):
    h, w = x.shape
    blk = flip.shape[0]
    nb = h // blk
    is_row0 = lax.broadcasted_iota(jnp.int32, (blk, w), 0) == 0
    blocks = [x[b * blk:(b + 1) * blk, :] for b in range(nb)]
    out = []
    for b in range(nb):
        inner = jnp.dot(flip, blocks[nb - b - 1], preferred_element_type=F32)
        first = blocks[(nb - b) % nb][0:1, :].astype(F32)
        out.append(jnp.where(is_row0, first, inner))
    return jnp.concatenate(out, axis=0)


def _fourier_kernel(n, u_ref, ch_ref, sh_ref, flip_ref, cc_ref, sc_ref, o_ref, a_scr, b_scr):
    h = n // 2
    is_row0 = lax.broadcasted_iota(jnp.int32, (h, FNET_W), 0) == 0
    u_top = u_ref[0:h, :].astype(F32)
    v = _flip_rows(u_ref[h:n, :], flip_ref[...])
    even = (u_top + jnp.where(is_row0, 0.0, v)).astype(BF16)
    odd = jnp.where(is_row0, 0.0, u_top - v).astype(BF16)
    pad_row0 = lax.broadcasted_iota(jnp.int32, (DFT_PAD_ROWS, FNET_W), 0) == 0
    mid = jnp.where(pad_row0, v[0:1, :], 0.0).astype(BF16)
    a_mid = []
    for g in range(N_FNET_GROUPS):
        cols = slice(g * FNET_GROUP, (g + 1) * FNET_GROUP)
        a_scr[:, cols] = jnp.dot(even[:, cols], cc_ref[...],
                                 preferred_element_type=F32).astype(BF16)
        b_scr[:, cols] = jnp.dot(odd[:, cols], sc_ref[...],
                                 preferred_element_type=F32).astype(BF16)
        a_mid.append(jnp.dot(mid[:, cols], cc_ref[...], preferred_element_type=F32)[0:1, :])
    a_mid = jnp.concatenate(a_mid, axis=1) * (n ** -0.5)
    p = jnp.dot(ch_ref[...], a_scr[...], preferred_element_type=F32)
    k_is_even = (lax.broadcasted_iota(jnp.int32, p.shape, 0) & 1) == 0
    p = p + jnp.where(k_is_even, a_mid, -a_mid)
    q = jnp.dot(sh_ref[...], b_scr[...], preferred_element_type=F32)
    o_ref[0:h, :] = (p[:h] - q).astype(BF16)
    z = jnp.where(is_row0, p[h:h + 1], p[:h] + q).astype(BF16)
    o_ref[h:n, :] = _flip_rows(z, flip_ref[...]).astype(BF16)


def _fourier(rest, batch):
    t = rest.shape[0]
    n = t // batch
    tables = [jnp.asarray(a).astype(BF16) for a in _dft_tables(n)]

    def whole(a):
        return pl.BlockSpec(a.shape, lambda b: (0, 0), pipeline_mode=pl.Buffered(1))

    return pl.pallas_call(
        functools.partial(_fourier_kernel, n),
        grid=(batch,),
        in_specs=[pl.BlockSpec((n, FNET_W), lambda b: (b, REST_UB_BLOCK))]
        + [whole(a) for a in tables],
        out_specs=pl.BlockSpec((n, FNET_W), lambda b: (b, 0)),
        out_shape=jax.ShapeDtypeStruct((t, FNET_W), BF16),
        scratch_shapes=[pltpu.VMEM((n // 2, FNET_W), BF16), pltpu.VMEM((n // 2, FNET_W), BF16)],
        compiler_params=pltpu.CompilerParams(
            dimension_semantics=("arbitrary",), vmem_limit_bytes=VMEM_LIMIT),
        name="fourier",
    )(rest, *tables)


def _merge_kernel(final, attn_ref, za_ref, fm_ref, zb_ref, ga_ref, gb_ref, x_ref, gate_ref,
                  wpa_ref, wpb_ref, wo_ref, *rest):
    def weight(w_ref):
        return pltpu.bitcast(w_ref[0], BF16)

    ta = (attn_ref[...].astype(F32) * _silu(za_ref[...].astype(F32))).astype(BF16)
    ya = jnp.dot(ta, weight(wpa_ref), preferred_element_type=F32)
    tb = (fm_ref[...].astype(F32) * _silu(zb_ref[...].astype(F32))).astype(BF16)
    yb = jnp.dot(tb, weight(wpb_ref), preferred_element_type=F32)
    mix = (jax.nn.sigmoid(ga_ref[...].astype(F32)) * ya
           + jax.nn.sigmoid(gb_ref[...].astype(F32)) * yb).astype(BF16)
    y = jnp.dot(mix, weight(wo_ref), preferred_element_type=F32)
    out = x_ref[...] + gate_ref[0] * y
    if final:
        fg_ref, o_ref = rest
        ms = jnp.mean(out * out, axis=-1, keepdims=True)
        o_ref[...] = out * lax.rsqrt(ms + EPS) * fg_ref[...]
    else:
        shift_ref, scale_ref, g_ref, o_ref, h_ref = rest
        o_ref[...] = out
        h_ref[...] = _modnorm(out, g_ref[0], scale_ref[0], shift_ref[0]).astype(BF16)


def _merge(rest, attn, fm, x2d, mods, layer, mod_row_of_tile, w_pa, w_pb, w_o, norm_g, final_g,
           tm):
    t = x2d.shape[0]
    final = layer == DEPTH - 1

    def rows(width, block=0):
        return pl.BlockSpec((tm, width), lambda i: (i, block))

    def weight(k):
        return pl.BlockSpec((1, k // 2, D_MODEL), lambda i: (layer, 0, 0),
                            pipeline_mode=pl.Buffered(1))

    in_specs = [
        rows(ATT_W), rows(ATT_W, REST_ZA_BLOCK), rows(FNET_W), rows(FNET_W, REST_ZB_BLOCK),
        rows(D_MODEL, REST_GA_BLOCK), rows(D_MODEL, REST_GB_BLOCK),
        rows(D_MODEL),
    ] + _mod_specs(layer, mod_row_of_tile, (MOD_GATE,)) + [
        weight(ATT_W), weight(FNET_W), weight(D_MODEL)]
    args = [attn, rest, fm, rest, rest, rest, x2d, mods, w_pa, w_pb, w_o]
    stream = jax.ShapeDtypeStruct((t, D_MODEL), F32)
    if final:
        in_specs += [pl.BlockSpec((1, D_MODEL), lambda i: (0, 0))]
        args += [final_g]
        out_specs, out_shape = rows(D_MODEL), stream
    else:
        in_specs += _mod_specs(layer + 1, mod_row_of_tile, (MOD_SHIFT, MOD_SCALE))
        in_specs += [pl.BlockSpec((1, 1, D_MODEL), lambda i: (layer + 1, 0, 0))]
        args += [mods, mods, norm_g]
        out_specs = [rows(D_MODEL), rows(D_MODEL)]
        out_shape = [stream, jax.ShapeDtypeStruct((t, D_MODEL), BF16)]
    return pl.pallas_call(
        functools.partial(_merge_kernel, final),
        grid=(t // tm,),
        in_specs=in_specs,
        out_specs=out_specs,
        out_shape=out_shape,
        compiler_params=pltpu.CompilerParams(
            dimension_semantics=("arbitrary",), vmem_limit_bytes=VMEM_LIMIT),
        name="merge",
    )(*args)


def _rope_tables(n_tokens):
    t = np.arange(n_tokens)
    pos = np.stack([t // GRID_W, t % GRID_W], axis=1).astype(np.float32)
    inv = (np.float32(ROPE_THETA) ** (-np.arange(ROPE_NFREQ, dtype=np.float32) / ROPE_NFREQ))
    ang = (pos[:, :, None] * inv[None, None, :]).astype(np.float32).astype(np.float64)
    cos, sin = np.cos(ang), np.sin(ang)
    zero = np.zeros_like(sin)
    c = np.stack([cos, cos], axis=2).reshape(n_tokens, HEAD_DIM)
    s1 = np.stack([-sin, zero], axis=2).reshape(n_tokens, HEAD_DIM)
    s2 = np.stack([zero, sin], axis=2).reshape(n_tokens, HEAD_DIM)
    return tuple(jnp.asarray(a.astype(np.float32)) for a in (c, s1, s2))


def _identity_tables(n_tokens):
    one = jnp.ones((n_tokens, HEAD_DIM), F32)
    zero = jnp.zeros((n_tokens, HEAD_DIM), F32)
    return one, zero, zero


def kernel(x, c, ctx, c_ctx, w_ada, b_ada, norm_g, w_in, q_norm_g, k_norm_g,
           w_proj_a, w_proj_b, w_out, final_g):
    batch, n_lat, d = x.shape
    n_ctx = ctx.shape[1]
    tm_norm, tm_merge, sub = 1024, 256, 512
    ctx_row = batch

    cvec = jnp.concatenate(
        [c, c_ctx[None], jnp.zeros((MOD_ROWS - batch - 1, d), F32)], axis=0)
    mods = _ada(cvec, w_ada, b_ada).reshape(DEPTH * MOD_ROWS * 3, 1, d)

    rope = _rope_tables(n_lat)
    no_rope = _identity_tables(batch * n_ctx)
    heads_per_tile = COL_TILE // HEAD_DIM
    qg = jnp.tile(q_norm_g * Q_SCALE, (1, heads_per_tile)).reshape(DEPTH, 1, COL_TILE)
    kg = jnp.tile(k_norm_g, (1, heads_per_tile)).reshape(DEPTH, 1, COL_TILE)
    merge_w = (w_proj_a, w_proj_b, w_out)
    norm_g3 = norm_g.reshape(DEPTH, 1, d)
    fg = final_g.reshape(1, d)

    def lat_row(tm):
        return lambda i: i // (n_lat // tm)

    def ctx_mod_row(i):
        return ctx_row

    xs = x.reshape(batch * n_lat, d)
    cs = ctx.reshape(batch * n_ctx, d)
    h = _modnorm_call(xs, mods, 0, lat_row(tm_norm), norm_g3, tm_norm)
    hc = _modnorm_call(cs, mods, 0, ctx_mod_row, norm_g3, tm_norm // 2)
    for l in range(DEPTH):
        last = l == DEPTH - 1
        lo, hi = (K_TILE, V_TILE + 1) if last else (0, N_COL_TILES)
        pc = _in_proj(hc, w_in, l, qg, kg, no_rope, lo, hi, batch * n_ctx, sub)
        casts = tuple(w.reshape(-1, d) for w in merge_w) if l == 0 else ()
        p = _in_proj(h, w_in, l, qg, kg, rope, 0, N_COL_TILES, n_lat, sub, casts)
        if l == 0:
            w_pa, w_pb, w_o = (c.reshape(DEPTH, -1, d) for c in p["casts"])

        attn = _attention(p["qt"], [(p["k"], p["vt"]), (pc["k"], pc["vt"])], batch, 256, 256)
        fm = _fourier(p["rest"], batch)
        merged = _merge(p["rest"], attn, fm, xs, mods, l, lat_row(tm_merge), w_pa, w_pb, w_o,
                        norm_g3, fg, tm_merge)
        if last:
            xs = merged
        else:
            xs, h = merged
            attn_c = _attention(pc["qt"], [(pc["k"], pc["vt"])], batch, n_ctx, n_ctx)
            fm_c = _fourier(pc["rest"], batch)
            cs, hc = _merge(pc["rest"], attn_c, fm_c, cs, mods, l, ctx_mod_row, w_pa, w_pb, w_o,
                            norm_g3, fg, tm_merge)
    return xs.reshape(batch, n_lat, d)
```

```python
import functools
import math

import numpy as np
import jax
import jax.numpy as jnp
from jax import lax
from jax.experimental import pallas as pl
from jax.experimental.pallas import tpu as pltpu

F32 = jnp.float32
BF16 = jnp.bfloat16

D_MODEL = 2048
DEPTH = 2
GRID_W = 64
HEAD_DIM = 128
ATT_W = (3 * D_MODEL) // 4
N_Q_HEADS = ATT_W // HEAD_DIM
N_KV_HEADS = 4
Q_PER_KV = N_Q_HEADS // N_KV_HEADS
KV_W = N_KV_HEADS * HEAD_DIM
FNET_W = D_MODEL // 4
N_FNET_GROUPS = 4
FNET_GROUP = FNET_W // N_FNET_GROUPS
ROPE_THETA = 10000.0
ROPE_NFREQ = HEAD_DIM // 4
EPS = 1e-6
IN_W = 2 * ATT_W + 2 * KV_W + 2 * FNET_W + 2 * D_MODEL
Q_SCALE = HEAD_DIM ** -0.5 * math.log2(math.e)

V7X_VMEM_BYTES = 64 * 1024 * 1024
V7X_VMEM_RESERVE_BYTES = 8 * 1024 * 1024
VMEM_LIMIT = V7X_VMEM_BYTES - V7X_VMEM_RESERVE_BYTES

COL_TILE = 512
N_COL_TILES = IN_W // COL_TILE
K_TILE = ATT_W // COL_TILE
V_TILE = K_TILE + 1
ZB_TILE = (2 * ATT_W + 2 * KV_W + FNET_W) // COL_TILE
REST_W = IN_W - ATT_W - 2 * KV_W
REST_ZA_BLOCK = 0
REST_UB_BLOCK = ATT_W // FNET_W
REST_GA_BLOCK = 1
REST_GB_BLOCK = 2
REST_ZB_BLOCK = REST_W // FNET_W - 1
MOD_ROWS = 8


def _silu(x):
    return x * jax.nn.sigmoid(x)


def _ada_kernel(cv_ref, w_ref, b_ref, o_ref):
    s = _silu(cv_ref[...]).astype(BF16)
    o_ref[0] = jnp.dot(s, w_ref[0].astype(BF16), preferred_element_type=F32) + b_ref[0]


def _ada(cvec, w_ada, b_ada):
    tn = 1024
    n = 3 * D_MODEL
    return pl.pallas_call(
        _ada_kernel,
        grid=(DEPTH, n // tn),
        in_specs=[
            pl.BlockSpec((MOD_ROWS, D_MODEL), lambda l, j: (0, 0)),
            pl.BlockSpec((1, D_MODEL, tn), lambda l, j: (l, 0, j)),
            pl.BlockSpec((1, 1, tn), lambda l, j: (l, 0, j)),
        ],
        out_specs=pl.BlockSpec((1, MOD_ROWS, tn), lambda l, j: (l, 0, j)),
        out_shape=jax.ShapeDtypeStruct((DEPTH, MOD_ROWS, n), F32),
        compiler_params=pltpu.CompilerParams(
            dimension_semantics=("arbitrary", "arbitrary"), vmem_limit_bytes=VMEM_LIMIT),
        name="ada",
    )(cvec, w_ada, b_ada.reshape(DEPTH, 1, n))


def _weight_tile(step):
    return jnp.where(step < ZB_TILE, step, jnp.where(step < N_COL_TILES - 1, step + 1, ZB_TILE))


def _head_norm_rope(acc, head_mean, g, c, s1, s2):
    ms = jnp.dot((acc * acc).astype(BF16), head_mean, preferred_element_type=F32)
    y = acc * lax.rsqrt(ms + EPS) * g
    outs = []
    for hh in range(COL_TILE // HEAD_DIM):
        yh = y[:, hh * HEAD_DIM:(hh + 1) * HEAD_DIM]
        outs.append(yh * c + pltpu.roll(yh, HEAD_DIM - ROPE_NFREQ, 1) * s1
                    + pltpu.roll(yh, ROPE_NFREQ, 1) * s2)
    return jnp.concatenate(outs, axis=-1)


def _modnorm(xf, g, scale, shift):
    ms = jnp.mean(xf * xf, axis=-1, keepdims=True)
    return xf * lax.rsqrt(ms + EPS) * (g * (1.0 + scale)) + shift


NORM_CHUNK = 16


def _modnorm_kernel(x_ref, shift_ref, scale_ref, g_ref, h_ref):
    gain = g_ref[0] * (1.0 + scale_ref[0])
    shift = shift_ref[0]

    def chunk(c, carry):
        rows = pl.ds(pl.multiple_of(c * NORM_CHUNK, NORM_CHUNK), NORM_CHUNK)
        xf = x_ref[rows, :]
        ms = jnp.mean(xf * xf, axis=-1, keepdims=True)
        h_ref[rows, :] = (xf * lax.rsqrt(ms + EPS) * gain + shift).astype(BF16)
        return carry

    lax.fori_loop(0, x_ref.shape[0] // NORM_CHUNK, chunk, None, unroll=8)


def _mod_specs(layer, mod_row_of_tile, parts):
    return [pl.BlockSpec((1, 1, D_MODEL),
                         lambda i, p=p: ((layer * MOD_ROWS + mod_row_of_tile(i)) * 3 + p, 0, 0))
            for p in parts]


MOD_SHIFT, MOD_SCALE, MOD_GATE = 0, 1, 2


def _modnorm_call(x2d, mods, layer, mod_row_of_tile, norm_g, tm):
    t = x2d.shape[0]
    return pl.pallas_call(
        _modnorm_kernel,
        grid=(t // tm,),
        in_specs=[pl.BlockSpec((tm, D_MODEL), lambda i: (i, 0))]
        + _mod_specs(layer, mod_row_of_tile, (MOD_SHIFT, MOD_SCALE))
        + [pl.BlockSpec((1, 1, D_MODEL), lambda i: (layer, 0, 0))],
        out_specs=pl.BlockSpec((tm, D_MODEL), lambda i: (i, 0)),
        out_shape=jax.ShapeDtypeStruct((t, D_MODEL), BF16),
        compiler_params=pltpu.CompilerParams(
            dimension_semantics=("arbitrary",), vmem_limit_bytes=VMEM_LIMIT),
        name="modnorm",
    )(x2d, mods, mods, norm_g)


def _pack_bf16_rows(x):
    return pltpu.bitcast(x.astype(BF16), jnp.uint32)


def _in_proj_kernel(step_lo, names, sub, n_casts, h_ref, w_ref, qg_ref, kg_ref, hm_ref,
                    c_ref, s1_ref, s2_ref, *rest):
    cast_in, out_refs = rest[:n_casts], rest[n_casts:len(rest) - n_casts]
    cast_out = rest[len(rest) - n_casts:]
    out = dict(zip(names, out_refs))
    step = pl.program_id(1) + step_lo
    row_tiles = [slice(r, r + sub) for r in range(0, h_ref.shape[0], sub)]

    def tiles():
        for src, dst in zip(cast_in, cast_out):
            dst[...] = _pack_bf16_rows(src[...])
        w = w_ref[0].astype(BF16)
        for rows in row_tiles:
            yield rows, jnp.dot(h_ref[rows, :], w, preferred_element_type=F32)

    def head_tiles(g_ref):
        for rows, acc in tiles():
            yield rows, _head_norm_rope(acc, hm_ref[...], g_ref[0], c_ref[rows, :],
                                        s1_ref[rows, :], s2_ref[rows, :])

    if "qt" in out:
        @pl.when(step < K_TILE)
        def _():
            for rows, qh in head_tiles(qg_ref):
                out["qt"][:, rows] = qh.T.astype(BF16)

    if "k" in out:
        @pl.when(step == K_TILE)
        def _():
            for rows, kh in head_tiles(kg_ref):
                out["k"][rows, :] = kh.astype(BF16)

    if "vt" in out:
        @pl.when(step == V_TILE)
        def _():
            for rows, acc in tiles():
                out["vt"][:, rows] = acc.T.astype(BF16)

    if "rest" in out:
        @pl.when(step > V_TILE)
        def _():
            for rows, acc in tiles():
                out["rest"][rows, :] = acc.astype(BF16)


CAST_STEPS = 64


def _in_proj(h, w_in, layer, q_g, k_g, tables, step_lo, step_hi, group, sub, casts=()):
    t = h.shape[0]
    n_steps = step_hi - step_lo
    assert not casts or (t // group) * n_steps >= CAST_STEPS

    def cast_spec(rows):
        return pl.BlockSpec((rows // CAST_STEPS, D_MODEL),
                            lambda i, j: (jnp.minimum(i * n_steps + j, CAST_STEPS - 1), 0))

    cast_in_specs = [cast_spec(a.shape[0]) for a in casts]
    cast_out_specs = [cast_spec(a.shape[0] // 2) for a in casts]
    cast_shapes = [jax.ShapeDtypeStruct((a.shape[0] // 2, a.shape[1]), jnp.uint32)
                   for a in casts]
    c_tab, s1_tab, s2_tab = tables
    n_rest = REST_W // COL_TILE
    heads_per_tile = COL_TILE // HEAD_DIM
    head_mean = jnp.asarray(
        np.kron(np.eye(heads_per_tile), np.full((HEAD_DIM, HEAD_DIM), 1.0 / HEAD_DIM)), BF16)
    specs = {
        "qt": (pl.BlockSpec((COL_TILE, group),
                            lambda i, j: (jnp.clip(j + step_lo, 0, K_TILE - 1), i)),
               jax.ShapeDtypeStruct((ATT_W, t), BF16), step_lo < K_TILE),
        "k": (pl.BlockSpec((group, COL_TILE), lambda i, j: (i, 0)),
              jax.ShapeDtypeStruct((t, KV_W), BF16), step_lo <= K_TILE < step_hi),
        "vt": (pl.BlockSpec((COL_TILE, group), lambda i, j: (0, i)),
               jax.ShapeDtypeStruct((KV_W, t), BF16), step_lo <= V_TILE < step_hi),
        "rest": (pl.BlockSpec((group, COL_TILE),
                              lambda i, j: (i, jnp.clip(j + step_lo - V_TILE - 1, 0, n_rest - 1))),
                 jax.ShapeDtypeStruct((t, REST_W), BF16), step_hi > V_TILE + 1),
    }
    names = tuple(n for n, (_, _, present) in specs.items() if present)
    tab_spec = pl.BlockSpec((group, HEAD_DIM), lambda i, j: (0, 0))
    gain_spec = pl.BlockSpec((1, 1, COL_TILE), lambda i, j: (layer, 0, 0))
    outs = pl.pallas_call(
        functools.partial(_in_proj_kernel, step_lo, names, sub, len(casts)),
        grid=(t // group, n_steps),
        in_specs=[
            pl.BlockSpec((group, D_MODEL), lambda i, j: (i, 0)),
            pl.BlockSpec((1, D_MODEL, COL_TILE),
                         lambda i, j: (layer, 0, _weight_tile(j + step_lo))),
            gain_spec, gain_spec,
            pl.BlockSpec((COL_TILE, COL_TILE), lambda i, j: (0, 0)),
            tab_spec, tab_spec, tab_spec,
        ] + cast_in_specs,
        out_specs=[specs[n][0] for n in names] + cast_out_specs,
        out_shape=[specs[n][1] for n in names] + cast_shapes,
        compiler_params=pltpu.CompilerParams(
            dimension_semantics=("arbitrary", "arbitrary"), vmem_limit_bytes=VMEM_LIMIT),
        name="in_proj",
    )(h, w_in, q_g, k_g, head_mean, c_tab, s1_tab, s2_tab, *casts)
    result = dict(zip(names, outs))
    if casts:
        result["casts"] = outs[len(names):]
    return result


ONES_ROWS = 16


def _col_reduce8(x, op):
    rows, w = x.shape
    return op(x.reshape(rows // 8, 8, w), axis=0)


def _attn_kernel(chunks, tq, q_ref, *rest):
    n_sets = 1 + max(c[0] for c in chunks)
    k_refs, vt_refs = rest[:n_sets], rest[n_sets:2 * n_sets]
    o_ref, s_scr, m_scr = rest[2 * n_sets:]
    width = Q_PER_KV * tq

    @pl.when(pl.program_id(0) == 0)
    def _():
        s_scr[...] = jnp.zeros_like(s_scr)
        m_scr[...] = jnp.zeros_like(m_scr)

    q_t = jnp.concatenate(
        [q_ref[i * HEAD_DIM:(i + 1) * HEAD_DIM, :] for i in range(Q_PER_KV)], axis=1)
    m_prev = m_scr[...]
    m_run = jnp.full((8, width), -jnp.inf, F32)
    acc = jnp.zeros((HEAD_DIM + ONES_ROWS, width), F32)
    for t, r0, rows, s0 in chunks:
        p = jnp.exp2(s_scr[s0:s0 + rows, :] - m_prev).astype(BF16)
        vt_ones = jnp.concatenate(
            [vt_refs[t][:, r0:r0 + rows], jnp.ones((ONES_ROWS, rows), BF16)], axis=0)
        acc = acc + jnp.dot(vt_ones, p, preferred_element_type=F32)
        s_new = jnp.dot(k_refs[t][r0:r0 + rows, :], q_t, preferred_element_type=F32)
        s_scr[s0:s0 + rows, :] = s_new
        m_run = jnp.maximum(m_run, _col_reduce8(s_new, jnp.max))
    m_scr[...] = m_run.max(axis=0, keepdims=True)
    o_t = acc[:HEAD_DIM] / acc[HEAD_DIM:HEAD_DIM + 1]
    for i in range(Q_PER_KV):
        o_ref[:, i * HEAD_DIM:(i + 1) * HEAD_DIM] = o_t[:, i * tq:(i + 1) * tq].T.astype(BF16)


def _attention(qt, kv_sets, batch, tq, key_chunk):
    t = qt.shape[1]
    nq = t // batch // tq
    n_tiles = batch * N_KV_HEADS * nq
    gw = Q_PER_KV * HEAD_DIM

    def tile(n):
        return n // (N_KV_HEADS * nq), (n // nq) % N_KV_HEADS, n % nq

    def score_tile(n):
        return tile(jnp.minimum(n, n_tiles - 1))

    def out_tile(n):
        return tile(jnp.maximum(n - 1, 0))

    def q_map(n):
        b, g, i = score_tile(n)
        return g, b * nq + i

    def o_map(n):
        b, g, i = out_tile(n)
        return b * nq + i, g

    def k_map(n):
        b, g, _ = score_tile(n)
        return b, g

    def vt_map(n):
        b, g, _ = out_tile(n)
        return g, b

    chunks, k_specs, vt_specs, s_rows = [], [], [], 0
    for s, (k, _) in enumerate(kv_sets):
        m_t = k.shape[0] // batch
        step = min(key_chunk, m_t)
        for r0 in range(0, m_t, step):
            chunks.append((s, r0, step, s_rows + r0))
        s_rows += m_t
        k_specs.append(pl.BlockSpec((m_t, HEAD_DIM), k_map))
        vt_specs.append(pl.BlockSpec((HEAD_DIM, m_t), vt_map))
    return pl.pallas_call(
        functools.partial(_attn_kernel, tuple(chunks), tq),
        grid=(n_tiles + 1,),
        in_specs=[pl.BlockSpec((gw, tq), q_map)] + k_specs + vt_specs,
        out_specs=pl.BlockSpec((tq, gw), o_map),
        out_shape=jax.ShapeDtypeStruct((t, ATT_W), BF16),
        scratch_shapes=[pltpu.VMEM((s_rows, Q_PER_KV * tq), F32),
                        pltpu.VMEM((1, Q_PER_KV * tq), F32)],
        compiler_params=pltpu.CompilerParams(
            dimension_semantics=("arbitrary",), vmem_limit_bytes=VMEM_LIMIT),
        name="attn",
    )(qt, *[k for k, _ in kv_sets], *[vt for _, vt in kv_sets])


DFT_PAD_ROWS = 16
FLIP_BLOCK = 256


def _dft_tables(n):
    h = n // 2
    k = np.arange(h + 1, dtype=np.int64)[:, None]
    j = np.arange(h, dtype=np.int64)[None, :]
    ang = 2.0 * np.pi * ((k * j) % n).astype(np.float64) / n
    c_half = np.zeros((h + DFT_PAD_ROWS, h))
    c_half[:h + 1] = np.cos(ang) / np.sqrt(n)
    s_half = np.sin(ang[:h]) / np.sqrt(n)
    blk = min(FLIP_BLOCK, h)
    flip = np.zeros((blk, blk))
    flip[np.arange(1, blk), blk - np.arange(1, blk)] = 1.0
    ch = np.arange(FNET_GROUP, dtype=np.int64)
    ang_c = 2.0 * np.pi * ((ch[:, None] * ch[None, :]) % FNET_GROUP) / FNET_GROUP
    cc, sc = np.cos(ang_c) / np.sqrt(FNET_GROUP), np.sin(ang_c) / np.sqrt(FNET_GROUP)
    return tuple(a.astype(np.float32) for a in (c_half, s_half, flip, cc, sc))


def _flip_rows(x, flip):
    h, w = x.shape
    blk = flip.shape[0]
    nb = h // blk
    is_row0 = lax.broadcasted_iota(jnp.int32, (blk, w), 0) == 0
    blocks = [x[b * blk:(b + 1) * blk, :] for b in range(nb)]
    out = []
    for b in range(nb):
        inner = jnp.dot(flip, blocks[nb - b - 1], preferred_element_type=F32)
        first = blocks[(nb - b) % nb][0:1, :].astype(F32)
        out.append(jnp.where(is_row0, first, inner))
    return jnp.concatenate(out, axis=0)


def _fourier_kernel(n, u_ref, ch_ref, sh_ref, flip_ref, cc_ref, sc_ref, o_ref, a_scr, b_scr):
    h = n // 2
    is_row0 = lax.broadcasted_iota(jnp.int32, (h, FNET_W), 0) == 0
    u_top = u_ref[0:h, :].astype(F32)
    v = _flip_rows(u_ref[h:n, :], flip_ref[...])
    even = (u_top + jnp.where(is_row0, 0.0, v)).astype(BF16)
    odd = jnp.where(is_row0, 0.0, u_top - v).astype(BF16)
    pad_row0 = lax.broadcasted_iota(jnp.int32, (DFT_PAD_ROWS, FNET_W), 0) == 0
    mid = jnp.where(pad_row0, v[0:1, :], 0.0).astype(BF16)
    a_mid = []
    for g in range(N_FNET_GROUPS):
        cols = slice(g * FNET_GROUP, (g + 1) * FNET_GROUP)
        a_scr[:, cols] = jnp.dot(even[:, cols], cc_ref[...],
                                 preferred_element_type=F32).astype(BF16)
        b_scr[:, cols] = jnp.dot(odd[:, cols], sc_ref[...],
                                 preferred_element_type=F32).astype(BF16)
        a_mid.append(jnp.dot(mid[:, cols], cc_ref[...], preferred_element_type=F32)[0:1, :])
    a_mid = jnp.concatenate(a_mid, axis=1) * (n ** -0.5)
    p = jnp.dot(ch_ref[...], a_scr[...], preferred_element_type=F32)
    k_is_even = (lax.broadcasted_iota(jnp.int32, p.shape, 0) & 1) == 0
    p = p + jnp.where(k_is_even, a_mid, -a_mid)
    q = jnp.dot(sh_ref[...], b_scr[...], preferred_element_type=F32)
    o_ref[0:h, :] = (p[:h] - q).astype(BF16)
    z = jnp.where(is_row0, p[h:h + 1], p[:h] + q).astype(BF16)
    o_ref[h:n, :] = _flip_rows(z, flip_ref[...]).astype(BF16)


def _fourier(rest, batch):
    t = rest.shape[0]
    n = t // batch
    tables = [jnp.asarray(a).astype(BF16) for a in _dft_tables(n)]

    def whole(a):
        return pl.BlockSpec(a.shape, lambda b: (0, 0), pipeline_mode=pl.Buffered(1))

    return pl.pallas_call(
        functools.partial(_fourier_kernel, n),
        grid=(batch,),
        in_specs=[pl.BlockSpec((n, FNET_W), lambda b: (b, REST_UB_BLOCK))]
        + [whole(a) for a in tables],
        out_specs=pl.BlockSpec((n, FNET_W), lambda b: (b, 0)),
        out_shape=jax.ShapeDtypeStruct((t, FNET_W), BF16),
        scratch_shapes=[pltpu.VMEM((n // 2, FNET_W), BF16), pltpu.VMEM((n // 2, FNET_W), BF16)],
        compiler_params=pltpu.CompilerParams(
            dimension_semantics=("arbitrary",), vmem_limit_bytes=VMEM_LIMIT),
        name="fourier",
    )(rest, *tables)


def _merge_kernel(final, attn_ref, za_ref, fm_ref, zb_ref, ga_ref, gb_ref, x_ref, gate_ref,
                  wpa_ref, wpb_ref, wo_ref, *rest):
    def weight(w_ref):
        return pltpu.bitcast(w_ref[0], BF16)

    ta = (attn_ref[...].astype(F32) * _silu(za_ref[...].astype(F32))).astype(BF16)
    ya = jnp.dot(ta, weight(wpa_ref), preferred_element_type=F32)
    tb = (fm_ref[...].astype(F32) * _silu(zb_ref[...].astype(F32))).astype(BF16)
    yb = jnp.dot(tb, weight(wpb_ref), preferred_element_type=F32)
    mix = (jax.nn.sigmoid(ga_ref[...].astype(F32)) * ya
           + jax.nn.sigmoid(gb_ref[...].astype(F32)) * yb).astype(BF16)
    y = jnp.dot(mix, weight(wo_ref), preferred_element_type=F32)
    out = x_ref[...] + gate_ref[0] * y
    if final:
        fg_ref, o_ref = rest
        ms = jnp.mean(out * out, axis=-1, keepdims=True)
        o_ref[...] = out * lax.rsqrt(ms + EPS) * fg_ref[...]
    else:
        shift_ref, scale_ref, g_ref, o_ref, h_ref = rest
        o_ref[...] = out
        h_ref[...] = _modnorm(out, g_ref[0], scale_ref[0], shift_ref[0]).astype(BF16)


def _merge(rest, attn, fm, x2d, mods, layer, mod_row_of_tile, w_pa, w_pb, w_o, norm_g, final_g,
           tm):
    t = x2d.shape[0]
    final = layer == DEPTH - 1

    def rows(width, block=0):
        return pl.BlockSpec((tm, width), lambda i: (i, block))

    def weight(k):
        return pl.BlockSpec((1, k // 2, D_MODEL), lambda i: (layer, 0, 0),
                            pipeline_mode=pl.Buffered(1))

    in_specs = [
        rows(ATT_W), rows(ATT_W, REST_ZA_BLOCK), rows(FNET_W), rows(FNET_W, REST_ZB_BLOCK),
        rows(D_MODEL, REST_GA_BLOCK), rows(D_MODEL, REST_GB_BLOCK),
        rows(D_MODEL),
    ] + _mod_specs(layer, mod_row_of_tile, (MOD_GATE,)) + [
        weight(ATT_W), weight(FNET_W), weight(D_MODEL)]
    args = [attn, rest, fm, rest, rest, rest, x2d, mods, w_pa, w_pb, w_o]
    stream = jax.ShapeDtypeStruct((t, D_MODEL), F32)
    if final:
        in_specs += [pl.BlockSpec((1, D_MODEL), lambda i: (0, 0))]
        args += [final_g]
        out_specs, out_shape = rows(D_MODEL), stream
    else:
        in_specs += _mod_specs(layer + 1, mod_row_of_tile, (MOD_SHIFT, MOD_SCALE))
        in_specs += [pl.BlockSpec((1, 1, D_MODEL), lambda i: (layer + 1, 0, 0))]
        args += [mods, mods, norm_g]
        out_specs = [rows(D_MODEL), rows(D_MODEL)]
        out_shape = [stream, jax.ShapeDtypeStruct((t, D_MODEL), BF16)]
    return pl.pallas_call(
        functools.partial(_merge_kernel, final),
        grid=(t // tm,),
        in_specs=in_specs,
        out_specs=out_specs,
        out_shape=out_shape,
        compiler_params=pltpu.CompilerParams(
            dimension_semantics=("arbitrary",), vmem_limit_bytes=VMEM_LIMIT),
        name="merge",
    )(*args)


def _rope_tables(n_tokens):
    t = np.arange(n_tokens)
    pos = np.stack([t // GRID_W, t % GRID_W], axis=1).astype(np.float32)
    inv = (np.float32(ROPE_THETA) ** (-np.arange(ROPE_NFREQ, dtype=np.float32) / ROPE_NFREQ))
    ang = (pos[:, :, None] * inv[None, None, :]).astype(np.float32).astype(np.float64)
    cos, sin = np.cos(ang), np.sin(ang)
    zero = np.zeros_like(sin)
    c = np.stack([cos, cos], axis=2).reshape(n_tokens, HEAD_DIM)
    s1 = np.stack([-sin, zero], axis=2).reshape(n_tokens, HEAD_DIM)
    s2 = np.stack([zero, sin], axis=2).reshape(n_tokens, HEAD_DIM)
    return tuple(jnp.asarray(a.astype(np.float32)) for a in (c, s1, s2))


def _identity_tables(n_tokens):
    one = jnp.ones((n_tokens, HEAD_DIM), F32)
    zero = jnp.zeros((n_tokens, HEAD_DIM), F32)
    return one, zero, zero


def kernel(x, c, ctx, c_ctx, w_ada, b_ada, norm_g, w_in, q_norm_g, k_norm_g,
           w_proj_a, w_proj_b, w_out, final_g):
    batch, n_lat, d = x.shape
    n_ctx = ctx.shape[1]
    tm_norm, tm_merge, sub = 1024, 256, 512
    ctx_row = batch

    cvec = jnp.concatenate(
        [c, c_ctx[None], jnp.zeros((MOD_ROWS - batch - 1, d), F32)], axis=0)
    mods = _ada(cvec, w_ada, b_ada).reshape(DEPTH * MOD_ROWS * 3, 1, d)

    rope = _rope_tables(n_lat)
    no_rope = _identity_tables(batch * n_ctx)
    heads_per_tile = COL_TILE // HEAD_DIM
    qg = jnp.tile(q_norm_g * Q_SCALE, (1, heads_per_tile)).reshape(DEPTH, 1, COL_TILE)
    kg = jnp.tile(k_norm_g, (1, heads_per_tile)).reshape(DEPTH, 1, COL_TILE)
    merge_w = (w_proj_a, w_proj_b, w_out)
    norm_g3 = norm_g.reshape(DEPTH, 1, d)
    fg = final_g.reshape(1, d)

    def lat_row(tm):
        return lambda i: i // (n_lat // tm)

    def ctx_mod_row(i):
        return ctx_row

    xs = x.reshape(batch * n_lat, d)
    cs = ctx.reshape(batch * n_ctx, d)
    h = _modnorm_call(xs, mods, 0, lat_row(tm_norm), norm_g3, tm_norm)
    hc = _modnorm_call(cs, mods, 0, ctx_mod_row, norm_g3, tm_norm // 2)
    for l in range(DEPTH):
        last = l == DEPTH - 1
        lo, hi = (K_TILE, V_TILE + 1) if last else (0, N_COL_TILES)
        pc = _in_proj(hc, w_in, l, qg, kg, no_rope, lo, hi, batch * n_ctx, sub)
        casts = tuple(w.reshape(-1, d) for w in merge_w) if l == 0 else ()
        p = _in_proj(h, w_in, l, qg, kg, rope, 0, N_COL_TILES, n_lat, sub, casts)
        if l == 0:
            w_pa, w_pb, w_o = (c.reshape(DEPTH, -1, d) for c in p["casts"])

        attn = _attention(p["qt"], [(p["k"], p["vt"]), (pc["k"], pc["vt"])], batch, 256, 256)
        fm = _fourier(p["rest"], batch)
        merged = _merge(p["rest"], attn, fm, xs, mods, l, lat_row(tm_merge), w_pa, w_pb, w_o,
                        norm_g3, fg, tm_merge)
        if last:
            xs = merged
        else:
            xs, h = merged
            attn_c = _attention(pc["qt"], [(pc["k"], pc["vt"])], batch, n_ctx, n_ctx)
            fm_c = _fourier(pc["rest"], batch)
            cs, hc = _merge(pc["rest"], attn_c, fm_c, cs, mods, l, ctx_mod_row, w_pa, w_pb, w_o,
                            norm_g3, fg, tm_merge)
    return xs.reshape(batch, n_lat, d)
```

```python
import functools
import math

import numpy as np
import jax
import jax.numpy as jnp
from jax import lax
from jax.experimental import pallas as pl
from jax.experimental.pallas import tpu as pltpu

F32 = jnp.float32
BF16 = jnp.bfloat16

D_MODEL = 2048
DEPTH = 2
GRID_W = 64
HEAD_DIM = 128
ATT_W = (3 * D_MODEL) // 4
N_Q_HEADS = ATT_W // HEAD_DIM
N_KV_HEADS = 4
Q_PER_KV = N_Q_HEADS // N_KV_HEADS
KV_W = N_KV_HEADS * HEAD_DIM
FNET_W = D_MODEL // 4
N_FNET_GROUPS = 4
FNET_GROUP = FNET_W // N_FNET_GROUPS
ROPE_THETA = 10000.0
ROPE_NFREQ = HEAD_DIM // 4
EPS = 1e-6
IN_W = 2 * ATT_W + 2 * KV_W + 2 * FNET_W + 2 * D_MODEL
Q_SCALE = HEAD_DIM ** -0.5 * math.log2(math.e)

V7X_VMEM_BYTES = 64 * 1024 * 1024
V7X_VMEM_RESERVE_BYTES = 8 * 1024 * 1024
VMEM_LIMIT = V7X_VMEM_BYTES - V7X_VMEM_RESERVE_BYTES

COL_TILE = 512
N_COL_TILES = IN_W // COL_TILE
K_TILE = ATT_W // COL_TILE
V_TILE = K_TILE + 1
ZB_TILE = (2 * ATT_W + 2 * KV_W + FNET_W) // COL_TILE
REST_W = IN_W - ATT_W - 2 * KV_W
REST_ZA_BLOCK = 0
REST_UB_BLOCK = ATT_W // FNET_W
REST_GA_BLOCK = 1
REST_GB_BLOCK = 2
REST_ZB_BLOCK = REST_W // FNET_W - 1
MOD_ROWS = 8


def _silu(x):
    return x * jax.nn.sigmoid(x)


def _ada_kernel(cv_ref, w_ref, b_ref, o_ref):
    s = _silu(cv_ref[...]).astype(BF16)
    o_ref[0] = jnp.dot(s, w_ref[0].astype(BF16), preferred_element_type=F32) + b_ref[0]


def _ada(cvec, w_ada, b_ada):
    tn = 1024
    n = 3 * D_MODEL
    return pl.pallas_call(
        _ada_kernel,
        grid=(DEPTH, n // tn),
        in_specs=[
            pl.BlockSpec((MOD_ROWS, D_MODEL), lambda l, j: (0, 0)),
            pl.BlockSpec((1, D_MODEL, tn), lambda l, j: (l, 0, j)),
            pl.BlockSpec((1, 1, tn), lambda l, j: (l, 0, j)),
        ],
        out_specs=pl.BlockSpec((1, MOD_ROWS, tn), lambda l, j: (l, 0, j)),
        out_shape=jax.ShapeDtypeStruct((DEPTH, MOD_ROWS, n), F32),
        compiler_params=pltpu.CompilerParams(
            dimension_semantics=("arbitrary", "arbitrary"), vmem_limit_bytes=VMEM_LIMIT),
        name="ada",
    )(cvec, w_ada, b_ada.reshape(DEPTH, 1, n))


def _weight_tile(step):
    return jnp.where(step < ZB_TILE, step, jnp.where(step < N_COL_TILES - 1, step + 1, ZB_TILE))


def _head_norm_rope(acc, head_mean, g, c, s1, s2):
    ms = jnp.dot((acc * acc).astype(BF16), head_mean, preferred_element_type=F32)
    y = acc * lax.rsqrt(ms + EPS) * g
    outs = []
    for hh in range(COL_TILE // HEAD_DIM):
        yh = y[:, hh * HEAD_DIM:(hh + 1) * HEAD_DIM]
        outs.append(yh * c + pltpu.roll(yh, HEAD_DIM - ROPE_NFREQ, 1) * s1
                    + pltpu.roll(yh, ROPE_NFREQ, 1) * s2)
    return jnp.concatenate(outs, axis=-1)


def _modnorm(xf, g, scale, shift):
    ms = jnp.mean(xf * xf, axis=-1, keepdims=True)
    return xf * lax.rsqrt(ms + EPS) * (g * (1.0 + scale)) + shift


def _modnorm_kernel(x_ref, shift_ref, scale_ref, g_ref, h_ref):
    h_ref[...] = _modnorm(x_ref[...], g_ref[0], scale_ref[0], shift_ref[0]).astype(BF16)


def _mod_specs(layer, mod_row_of_tile, parts):
    return [pl.BlockSpec((1, 1, D_MODEL),
                         lambda i, p=p: ((layer * MOD_ROWS + mod_row_of_tile(i)) * 3 + p, 0, 0))
            for p in parts]


MOD_SHIFT, MOD_SCALE, MOD_GATE = 0, 1, 2


def _modnorm_call(x2d, mods, layer, mod_row_of_tile, norm_g, tm):
    t = x2d.shape[0]
    return pl.pallas_call(
        _modnorm_kernel,
        grid=(t // tm,),
        in_specs=[pl.BlockSpec((tm, D_MODEL), lambda i: (i, 0))]
        + _mod_specs(layer, mod_row_of_tile, (MOD_SHIFT, MOD_SCALE))
        + [pl.BlockSpec((1, 1, D_MODEL), lambda i: (layer, 0, 0))],
        out_specs=pl.BlockSpec((tm, D_MODEL), lambda i: (i, 0)),
        out_shape=jax.ShapeDtypeStruct((t, D_MODEL), BF16),
        compiler_params=pltpu.CompilerParams(
            dimension_semantics=("arbitrary",), vmem_limit_bytes=VMEM_LIMIT),
        name="modnorm",
    )(x2d, mods, mods, norm_g)


def _pack_bf16_rows(x):
    return pltpu.bitcast(x.astype(BF16), jnp.uint32)


def _in_proj_kernel(step_lo, names, sub, n_casts, h_ref, w_ref, qg_ref, kg_ref, hm_ref,
                    c_ref, s1_ref, s2_ref, *rest):
    cast_in, out_refs = rest[:n_casts], rest[n_casts:len(rest) - n_casts]
    cast_out = rest[len(rest) - n_casts:]
    out = dict(zip(names, out_refs))
    step = pl.program_id(1) + step_lo
    row_tiles = [slice(r, r + sub) for r in range(0, h_ref.shape[0], sub)]

    def tiles():
        for src, dst in zip(cast_in, cast_out):
            dst[...] = _pack_bf16_rows(src[...])
        w = w_ref[0].astype(BF16)
        for rows in row_tiles:
            yield rows, jnp.dot(h_ref[rows, :], w, preferred_element_type=F32)

    def head_tiles(g_ref):
        for rows, acc in tiles():
            yield rows, _head_norm_rope(acc, hm_ref[...], g_ref[0], c_ref[rows, :],
                                        s1_ref[rows, :], s2_ref[rows, :])

    if "qt" in out:
        @pl.when(step < K_TILE)
        def _():
            for rows, qh in head_tiles(qg_ref):
                out["qt"][:, rows] = qh.T.astype(BF16)

    if "k" in out:
        @pl.when(step == K_TILE)
        def _():
            for rows, kh in head_tiles(kg_ref):
                out["k"][rows, :] = kh.astype(BF16)

    if "vt" in out:
        @pl.when(step == V_TILE)
        def _():
            for rows, acc in tiles():
                out["vt"][:, rows] = acc.T.astype(BF16)

    if "rest" in out:
        @pl.when(step > V_TILE)
        def _():
            for rows, acc in tiles():
                out["rest"][rows, :] = acc.astype(BF16)


CAST_STEPS = 64


def _in_proj(h, w_in, layer, q_g, k_g, tables, step_lo, step_hi, group, sub, casts=()):
    t = h.shape[0]
    n_steps = step_hi - step_lo
    assert not casts or (t // group) * n_steps >= CAST_STEPS

    def cast_spec(rows):
        return pl.BlockSpec((rows // CAST_STEPS, D_MODEL),
                            lambda i, j: (jnp.minimum(i * n_steps + j, CAST_STEPS - 1), 0))

    cast_in_specs = [cast_spec(a.shape[0]) for a in casts]
    cast_out_specs = [cast_spec(a.shape[0] // 2) for a in casts]
    cast_shapes = [jax.ShapeDtypeStruct((a.shape[0] // 2, a.shape[1]), jnp.uint32)
                   for a in casts]
    c_tab, s1_tab, s2_tab = tables
    n_rest = REST_W // COL_TILE
    heads_per_tile = COL_TILE // HEAD_DIM
    head_mean = jnp.asarray(
        np.kron(np.eye(heads_per_tile), np.full((HEAD_DIM, HEAD_DIM), 1.0 / HEAD_DIM)), BF16)
    specs = {
        "qt": (pl.BlockSpec((COL_TILE, group),
                            lambda i, j: (jnp.clip(j + step_lo, 0, K_TILE - 1), i)),
               jax.ShapeDtypeStruct((ATT_W, t), BF16), step_lo < K_TILE),
        "k": (pl.BlockSpec((group, COL_TILE), lambda i, j: (i, 0)),
              jax.ShapeDtypeStruct((t, KV_W), BF16), step_lo <= K_TILE < step_hi),
        "vt": (pl.BlockSpec((COL_TILE, group), lambda i, j: (0, i)),
               jax.ShapeDtypeStruct((KV_W, t), BF16), step_lo <= V_TILE < step_hi),
        "rest": (pl.BlockSpec((group, COL_TILE),
                              lambda i, j: (i, jnp.clip(j + step_lo - V_TILE - 1, 0, n_rest - 1))),
                 jax.ShapeDtypeStruct((t, REST_W), BF16), step_hi > V_TILE + 1),
    }
    names = tuple(n for n, (_, _, present) in specs.items() if present)
    tab_spec = pl.BlockSpec((group, HEAD_DIM), lambda i, j: (0, 0))
    gain_spec = pl.BlockSpec((1, 1, COL_TILE), lambda i, j: (layer, 0, 0))
    outs = pl.pallas_call(
        functools.partial(_in_proj_kernel, step_lo, names, sub, len(casts)),
        grid=(t // group, n_steps),
        in_specs=[
            pl.BlockSpec((group, D_MODEL), lambda i, j: (i, 0)),
            pl.BlockSpec((1, D_MODEL, COL_TILE),
                         lambda i, j: (layer, 0, _weight_tile(j + step_lo))),
            gain_spec, gain_spec,
            pl.BlockSpec((COL_TILE, COL_TILE), lambda i, j: (0, 0)),
            tab_spec, tab_spec, tab_spec,
        ] + cast_in_specs,
        out_specs=[specs[n][0] for n in names] + cast_out_specs,
        out_shape=[specs[n][1] for n in names] + cast_shapes,
        compiler_params=pltpu.CompilerParams(
            dimension_semantics=("arbitrary", "arbitrary"), vmem_limit_bytes=VMEM_LIMIT),
        name="in_proj",
    )(h, w_in, q_g, k_g, head_mean, c_tab, s1_tab, s2_tab, *casts)
    result = dict(zip(names, outs))
    if casts:
        result["casts"] = outs[len(names):]
    return result


ONES_ROWS = 16


def _col_reduce8(x, op):
    rows, w = x.shape
    return op(x.reshape(rows // 8, 8, w), axis=0)


def _attn_kernel(chunks, tq, q_ref, *rest):
    n_sets = 1 + max(c[0] for c in chunks)
    k_refs, vt_refs = rest[:n_sets], rest[n_sets:2 * n_sets]
    o_ref, s_scr, m_scr = rest[2 * n_sets:]
    width = Q_PER_KV * tq

    @pl.when(pl.program_id(0) == 0)
    def _():
        s_scr[...] = jnp.zeros_like(s_scr)
        m_scr[...] = jnp.zeros_like(m_scr)

    q_t = jnp.concatenate(
        [q_ref[i * HEAD_DIM:(i + 1) * HEAD_DIM, :] for i in range(Q_PER_KV)], axis=1)
    m_prev = m_scr[...]
    m_run = jnp.full((8, width), -jnp.inf, F32)
    acc = jnp.zeros((HEAD_DIM + ONES_ROWS, width), F32)
    for t, r0, rows, s0 in chunks:
        p = jnp.exp2(s_scr[s0:s0 + rows, :] - m_prev).astype(BF16)
        vt_ones = jnp.concatenate(
            [vt_refs[t][:, r0:r0 + rows], jnp.ones((ONES_ROWS, rows), BF16)], axis=0)
        acc = acc + jnp.dot(vt_ones, p, preferred_element_type=F32)
        s_new = jnp.dot(k_refs[t][r0:r0 + rows, :], q_t, preferred_element_type=F32)
        s_scr[s0:s0 + rows, :] = s_new
        m_run = jnp.maximum(m_run, _col_reduce8(s_new, jnp.max))
    m_scr[...] = m_run.max(axis=0, keepdims=True)
    o_t = acc[:HEAD_DIM] / acc[HEAD_DIM:HEAD_DIM + 1]
    for i in range(Q_PER_KV):
        o_ref[:, i * HEAD_DIM:(i + 1) * HEAD_DIM] = o_t[:, i * tq:(i + 1) * tq].T.astype(BF16)


def _attention(qt, kv_sets, batch, tq, key_chunk):
    t = qt.shape[1]
    nq = t // batch // tq
    n_tiles = batch * N_KV_HEADS * nq
    gw = Q_PER_KV * HEAD_DIM

    def tile(n):
        return n // (N_KV_HEADS * nq), (n // nq) % N_KV_HEADS, n % nq

    def score_tile(n):
        return tile(jnp.minimum(n, n_tiles - 1))

    def out_tile(n):
        return tile(jnp.maximum(n - 1, 0))

    def q_map(n):
        b, g, i = score_tile(n)
        return g, b * nq + i

    def o_map(n):
        b, g, i = out_tile(n)
        return b * nq + i, g

    def k_map(n):
        b, g, _ = score_tile(n)
        return b, g

    def vt_map(n):
        b, g, _ = out_tile(n)
        return g, b

    chunks, k_specs, vt_specs, s_rows = [], [], [], 0
    for s, (k, _) in enumerate(kv_sets):
        m_t = k.shape[0] // batch
        step = min(key_chunk, m_t)
        for r0 in range(0, m_t, step):
            chunks.append((s, r0, step, s_rows + r0))
        s_rows += m_t
        k_specs.append(pl.BlockSpec((m_t, HEAD_DIM), k_map))
        vt_specs.append(pl.BlockSpec((HEAD_DIM, m_t), vt_map))
    return pl.pallas_call(
        functools.partial(_attn_kernel, tuple(chunks), tq),
        grid=(n_tiles + 1,),
        in_specs=[pl.BlockSpec((gw, tq), q_map)] + k_specs + vt_specs,
        out_specs=pl.BlockSpec((tq, gw), o_map),
        out_shape=jax.ShapeDtypeStruct((t, ATT_W), BF16),
        scratch_shapes=[pltpu.VMEM((s_rows, Q_PER_KV * tq), F32),
                        pltpu.VMEM((1, Q_PER_KV * tq), F32)],
        compiler_params=pltpu.CompilerParams(
            dimension_semantics=("arbitrary",), vmem_limit_bytes=VMEM_LIMIT),
        name="attn",
    )(qt, *[k for k, _ in kv_sets], *[vt for _, vt in kv_sets])


DFT_PAD_ROWS = 16
FLIP_BLOCK = 256


def _dft_tables(n):
    h = n // 2
    k = np.arange(h + 1, dtype=np.int64)[:, None]
    j = np.arange(h, dtype=np.int64)[None, :]
    ang = 2.0 * np.pi * ((k * j) % n).astype(np.float64) / n
    c_half = np.zeros((h + DFT_PAD_ROWS, h))
    c_half[:h + 1] = np.cos(ang) / np.sqrt(n)
    s_half = np.sin(ang[:h]) / np.sqrt(n)
    blk = min(FLIP_BLOCK, h)
    flip = np.zeros((blk, blk))
    flip[np.arange(1, blk), blk - np.arange(1, blk)] = 1.0
    ch = np.arange(FNET_GROUP, dtype=np.int64)
    ang_c = 2.0 * np.pi * ((ch[:, None] * ch[None, :]) % FNET_GROUP) / FNET_GROUP
    cc, sc = np.cos(ang_c) / np.sqrt(FNET_GROUP), np.sin(ang_c) / np.sqrt(FNET_GROUP)
    return tuple(a.astype(np.float32) for a in (c_half, s_half, flip, cc, sc))


def _flip_rows(x, flip):
    h, w = x.shape
    blk = flip.shape[0]
    nb = h // blk
    is_row0 = lax.broadcasted_iota(jnp.int32, (blk, w), 0) == 0
    blocks = [x[b * blk:(b + 1) * blk, :] for b in range(nb)]
    out = []
    for b in range(nb):
        inner = jnp.dot(flip, blocks[nb - b - 1], preferred_element_type=F32)
        first = blocks[(nb - b) % nb][0:1, :].astype(F32)
        out.append(jnp.where(is_row0, first, inner))
    return jnp.concatenate(out, axis=0)


def _fourier_kernel(n, u_ref, ch_ref, sh_ref, flip_ref, cc_ref, sc_ref, o_ref, a_scr, b_scr):
    h = n // 2
    is_row0 = lax.broadcasted_iota(jnp.int32, (h, FNET_W), 0) == 0
    u_top = u_ref[0:h, :].astype(F32)
    v = _flip_rows(u_ref[h:n, :], flip_ref[...])
    even = (u_top + jnp.where(is_row0, 0.0, v)).astype(BF16)
    odd = jnp.where(is_row0, 0.0, u_top - v).astype(BF16)
    pad_row0 = lax.broadcasted_iota(jnp.int32, (DFT_PAD_ROWS, FNET_W), 0) == 0
    mid = jnp.where(pad_row0, v[0:1, :], 0.0).astype(BF16)
    a_mid = []
    for g in range(N_FNET_GROUPS):
        cols = slice(g * FNET_GROUP, (g + 1) * FNET_GROUP)
        a_scr[:, cols] = jnp.dot(even[:, cols], cc_ref[...],
                                 preferred_element_type=F32).astype(BF16)
        b_scr[:, cols] = jnp.dot(odd[:, cols], sc_ref[...],
                                 preferred_element_type=F32).astype(BF16)
        a_mid.append(jnp.dot(mid[:, cols], cc_ref[...], preferred_element_type=F32)[0:1, :])
    a_mid = jnp.concatenate(a_mid, axis=1) * (n ** -0.5)
    p = jnp.dot(ch_ref[...], a_scr[...], preferred_element_type=F32)
    k_is_even = (lax.broadcasted_iota(jnp.int32, p.shape, 0) & 1) == 0
    p = p + jnp.where(k_is_even, a_mid, -a_mid)
    q = jnp.dot(sh_ref[...], b_scr[...], preferred_element_type=F32)
    o_ref[0:h, :] = (p[:h] - q).astype(BF16)
    z = jnp.where(is_row0, p[h:h + 1], p[:h] + q).astype(BF16)
    o_ref[h:n, :] = _flip_rows(z, flip_ref[...]).astype(BF16)


def _fourier(rest, batch):
    t = rest.shape[0]
    n = t // batch
    tables = [jnp.asarray(a).astype(BF16) for a in _dft_tables(n)]

    def whole(a):
        return pl.BlockSpec(a.shape, lambda b: (0, 0), pipeline_mode=pl.Buffered(1))

    return pl.pallas_call(
        functools.partial(_fourier_kernel, n),
        grid=(batch,),
        in_specs=[pl.BlockSpec((n, FNET_W), lambda b: (b, REST_UB_BLOCK))]
        + [whole(a) for a in tables],
        out_specs=pl.BlockSpec((n, FNET_W), lambda b: (b, 0)),
        out_shape=jax.ShapeDtypeStruct((t, FNET_W), BF16),
        scratch_shapes=[pltpu.VMEM((n // 2, FNET_W), BF16), pltpu.VMEM((n // 2, FNET_W), BF16)],
        compiler_params=pltpu.CompilerParams(
            dimension_semantics=("arbitrary",), vmem_limit_bytes=VMEM_LIMIT),
        name="fourier",
    )(rest, *tables)


def _merge_kernel(final, attn_ref, za_ref, fm_ref, zb_ref, ga_ref, gb_ref, x_ref, gate_ref,
                  wpa_ref, wpb_ref, wo_ref, *rest):
    def weight(w_ref):
        return pltpu.bitcast(w_ref[0], BF16)

    ta = (attn_ref[...].astype(F32) * _silu(za_ref[...].astype(F32))).astype(BF16)
    ya = jnp.dot(ta, weight(wpa_ref), preferred_element_type=F32)
    tb = (fm_ref[...].astype(F32) * _silu(zb_ref[...].astype(F32))).astype(BF16)
    yb = jnp.dot(tb, weight(wpb_ref), preferred_element_type=F32)
    mix = (jax.nn.sigmoid(ga_ref[...].astype(F32)) * ya
           + jax.nn.sigmoid(gb_ref[...].astype(F32)) * yb).astype(BF16)
    y = jnp.dot(mix, weight(wo_ref), preferred_element_type=F32)
    out = x_ref[...] + gate_ref[0] * y
    if final:
        fg_ref, o_ref = rest
        ms = jnp.mean(out * out, axis=-1, keepdims=True)
        o_ref[...] = out * lax.rsqrt(ms + EPS) * fg_ref[...]
    else:
        shift_ref, scale_ref, g_ref, o_ref, h_ref = rest
        o_ref[...] = out
        h_ref[...] = _modnorm(out, g_ref[0], scale_ref[0], shift_ref[0]).astype(BF16)


def _merge(rest, attn, fm, x2d, mods, layer, mod_row_of_tile, w_pa, w_pb, w_o, norm_g, final_g,
           tm):
    t = x2d.shape[0]
    final = layer == DEPTH - 1

    def rows(width, block=0):
        return pl.BlockSpec((tm, width), lambda i: (i, block))

    def weight(k):
        return pl.BlockSpec((1, k // 2, D_MODEL), lambda i: (layer, 0, 0),
                            pipeline_mode=pl.Buffered(1))

    in_specs = [
        rows(ATT_W), rows(ATT_W, REST_ZA_BLOCK), rows(FNET_W), rows(FNET_W, REST_ZB_BLOCK),
        rows(D_MODEL, REST_GA_BLOCK), rows(D_MODEL, REST_GB_BLOCK),
        rows(D_MODEL),
    ] + _mod_specs(layer, mod_row_of_tile, (MOD_GATE,)) + [
        weight(ATT_W), weight(FNET_W), weight(D_MODEL)]
    args = [attn, rest, fm, rest, rest, rest, x2d, mods, w_pa, w_pb, w_o]
    stream = jax.ShapeDtypeStruct((t, D_MODEL), F32)
    if final:
        in_specs += [pl.BlockSpec((1, D_MODEL), lambda i: (0, 0))]
        args += [final_g]
        out_specs, out_shape = rows(D_MODEL), stream
    else:
        in_specs += _mod_specs(layer + 1, mod_row_of_tile, (MOD_SHIFT, MOD_SCALE))
        in_specs += [pl.BlockSpec((1, 1, D_MODEL), lambda i: (layer + 1, 0, 0))]
        args += [mods, mods, norm_g]
        out_specs = [rows(D_MODEL), rows(D_MODEL)]
        out_shape = [stream, jax.ShapeDtypeStruct((t, D_MODEL), BF16)]
    return pl.pallas_call(
        functools.partial(_merge_kernel, final),
        grid=(t // tm,),
        in_specs=in_specs,
        out_specs=out_specs,
        out_shape=out_shape,
        compiler_params=pltpu.CompilerParams(
            dimension_semantics=("arbitrary",), vmem_limit_bytes=VMEM_LIMIT),
        name="merge",
    )(*args)


def _rope_tables(n_tokens):
    t = np.arange(n_tokens)
    pos = np.stack([t // GRID_W, t % GRID_W], axis=1).astype(np.float32)
    inv = (np.float32(ROPE_THETA) ** (-np.arange(ROPE_NFREQ, dtype=np.float32) / ROPE_NFREQ))
    ang = (pos[:, :, None] * inv[None, None, :]).astype(np.float32).astype(np.float64)
    cos, sin = np.cos(ang), np.sin(ang)
    zero = np.zeros_like(sin)
    c = np.stack([cos, cos], axis=2).reshape(n_tokens, HEAD_DIM)
    s1 = np.stack([-sin, zero], axis=2).reshape(n_tokens, HEAD_DIM)
    s2 = np.stack([zero, sin], axis=2).reshape(n_tokens, HEAD_DIM)
    return tuple(jnp.asarray(a.astype(np.float32)) for a in (c, s1, s2))


def _identity_tables(n_tokens):
    one = jnp.ones((n_tokens, HEAD_DIM), F32)
    zero = jnp.zeros((n_tokens, HEAD_DIM), F32)
    return one, zero, zero


def kernel(x, c, ctx, c_ctx, w_ada, b_ada, norm_g, w_in, q_norm_g, k_norm_g,
           w_proj_a, w_proj_b, w_out, final_g):
    batch, n_lat, d = x.shape
    n_ctx = ctx.shape[1]
    tm_norm, tm_merge, sub = 1024, 256, 512
    ctx_row = batch

    cvec = jnp.concatenate(
        [c, c_ctx[None], jnp.zeros((MOD_ROWS - batch - 1, d), F32)], axis=0)
    mods = _ada(cvec, w_ada, b_ada).reshape(DEPTH * MOD_ROWS * 3, 1, d)

    rope = _rope_tables(n_lat)
    no_rope = _identity_tables(batch * n_ctx)
    heads_per_tile = COL_TILE // HEAD_DIM
    qg = jnp.tile(q_norm_g * Q_SCALE, (1, heads_per_tile)).reshape(DEPTH, 1, COL_TILE)
    kg = jnp.tile(k_norm_g, (1, heads_per_tile)).reshape(DEPTH, 1, COL_TILE)
    merge_w = (w_proj_a, w_proj_b, w_out)
    norm_g3 = norm_g.reshape(DEPTH, 1, d)
    fg = final_g.reshape(1, d)

    def lat_row(tm):
        return lambda i: i // (n_lat // tm)

    def ctx_mod_row(i):
        return ctx_row

    xs = x.reshape(batch * n_lat, d)
    cs = ctx.reshape(batch * n_ctx, d)
    h = _modnorm_call(xs, mods, 0, lat_row(tm_norm), norm_g3, tm_norm)
    hc = _modnorm_call(cs, mods, 0, ctx_mod_row, norm_g3, tm_norm // 2)
    for l in range(DEPTH):
        last = l == DEPTH - 1
        lo, hi = (K_TILE, V_TILE + 1) if last else (0, N_COL_TILES)
        pc = _in_proj(hc, w_in, l, qg, kg, no_rope, lo, hi, batch * n_ctx, sub)
        casts = tuple(w.reshape(-1, d) for w in merge_w) if l == 0 else ()
        p = _in_proj(h, w_in, l, qg, kg, rope, 0, N_COL_TILES, n_lat, sub, casts)
        if l == 0:
            w_pa, w_pb, w_o = (c.reshape(DEPTH, -1, d) for c in p["casts"])

        attn = _attention(p["qt"], [(p["k"], p["vt"]), (pc["k"], pc["vt"])], batch, 256, 256)
        fm = _fourier(p["rest"], batch)
        merged = _merge(p["rest"], attn, fm, xs, mods, l, lat_row(tm_merge), w_pa, w_pb, w_o,
                        norm_g3, fg, tm_merge)
        if last:
            xs = merged
        else:
            xs, h = merged
            attn_c = _attention(pc["qt"], [(pc["k"], pc["vt"])], batch, n_ctx, n_ctx)
            fm_c = _fourier(pc["rest"], batch)
            cs, hc = _merge(pc["rest"], attn_c, fm_c, cs, mods, l, ctx_mod_row, w_pa, w_pb, w_o,
                            norm_g3, fg, tm_merge)
    return xs.reshape(batch, n_lat, d)
```

```python
import functools
import math

import numpy as np
import jax
import jax.numpy as jnp
from jax import lax
from jax.experimental import pallas as pl
from jax.experimental.pallas import tpu as pltpu

F32 = jnp.float32
BF16 = jnp.bfloat16

D_MODEL = 2048
DEPTH = 2
GRID_W = 64
HEAD_DIM = 128
ATT_W = (3 * D_MODEL) // 4
N_Q_HEADS = ATT_W // HEAD_DIM
N_KV_HEADS = 4
Q_PER_KV = N_Q_HEADS // N_KV_HEADS
KV_W = N_KV_HEADS * HEAD_DIM
FNET_W = D_MODEL // 4
N_FNET_GROUPS = 4
FNET_GROUP = FNET_W // N_FNET_GROUPS
ROPE_THETA = 10000.0
ROPE_NFREQ = HEAD_DIM // 4
EPS = 1e-6
IN_W = 2 * ATT_W + 2 * KV_W + 2 * FNET_W + 2 * D_MODEL
Q_SCALE = HEAD_DIM ** -0.5 * math.log2(math.e)

V7X_VMEM_BYTES = 64 * 1024 * 1024
V7X_VMEM_RESERVE_BYTES = 8 * 1024 * 1024
VMEM_LIMIT = V7X_VMEM_BYTES - V7X_VMEM_RESERVE_BYTES

COL_TILE = 512
N_COL_TILES = IN_W // COL_TILE
K_TILE = ATT_W // COL_TILE
V_TILE = K_TILE + 1
ZB_TILE = (2 * ATT_W + 2 * KV_W + FNET_W) // COL_TILE
REST_W = IN_W - ATT_W - 2 * KV_W
REST_ZA_BLOCK = 0
REST_UB_BLOCK = ATT_W // FNET_W
REST_GA_BLOCK = 1
REST_GB_BLOCK = 2
REST_ZB_BLOCK = REST_W // FNET_W - 1
MOD_ROWS = 8


def _silu(x):
    return x * jax.nn.sigmoid(x)


def _ada_kernel(cv_ref, w_ref, b_ref, o_ref):
    s = _silu(cv_ref[...]).astype(BF16)
    o_ref[0] = jnp.dot(s, w_ref[0].astype(BF16), preferred_element_type=F32) + b_ref[0]


def _ada(cvec, w_ada, b_ada):
    tn = 1024
    n = 3 * D_MODEL
    return pl.pallas_call(
        _ada_kernel,
        grid=(DEPTH, n // tn),
        in_specs=[
            pl.BlockSpec((MOD_ROWS, D_MODEL), lambda l, j: (0, 0)),
            pl.BlockSpec((1, D_MODEL, tn), lambda l, j: (l, 0, j)),
            pl.BlockSpec((1, 1, tn), lambda l, j: (l, 0, j)),
        ],
        out_specs=pl.BlockSpec((1, MOD_ROWS, tn), lambda l, j: (l, 0, j)),
        out_shape=jax.ShapeDtypeStruct((DEPTH, MOD_ROWS, n), F32),
        compiler_params=pltpu.CompilerParams(
            dimension_semantics=("arbitrary", "arbitrary"), vmem_limit_bytes=VMEM_LIMIT),
        name="ada",
    )(cvec, w_ada, b_ada.reshape(DEPTH, 1, n))


def _weight_tile(step):
    return jnp.where(step < ZB_TILE, step, jnp.where(step < N_COL_TILES - 1, step + 1, ZB_TILE))


def _head_norm_rope(acc, head_mean, g, c, s1, s2):
    ms = jnp.dot((acc * acc).astype(BF16), head_mean, preferred_element_type=F32)
    y = acc * lax.rsqrt(ms + EPS) * g
    outs = []
    for hh in range(COL_TILE // HEAD_DIM):
        yh = y[:, hh * HEAD_DIM:(hh + 1) * HEAD_DIM]
        outs.append(yh * c + pltpu.roll(yh, HEAD_DIM - ROPE_NFREQ, 1) * s1
                    + pltpu.roll(yh, ROPE_NFREQ, 1) * s2)
    return jnp.concatenate(outs, axis=-1)


def _modnorm(xf, g, scale, shift):
    ms = jnp.mean(xf * xf, axis=-1, keepdims=True)
    return xf * lax.rsqrt(ms + EPS) * (g * (1.0 + scale)) + shift


NORM_CHUNK = 16


def _modnorm_kernel(x_ref, shift_ref, scale_ref, g_ref, h_ref):
    gain = g_ref[0] * (1.0 + scale_ref[0])
    shift = shift_ref[0]

    def chunk(c, carry):
        rows = pl.ds(pl.multiple_of(c * NORM_CHUNK, NORM_CHUNK), NORM_CHUNK)
        xf = x_ref[rows, :]
        ms = jnp.mean(xf * xf, axis=-1, keepdims=True)
        h_ref[rows, :] = (xf * lax.rsqrt(ms + EPS) * gain + shift).astype(BF16)
        return carry

    lax.fori_loop(0, x_ref.shape[0] // NORM_CHUNK, chunk, None, unroll=8)


def _mod_specs(layer, mod_row_of_tile, parts):
    return [pl.BlockSpec((1, 1, D_MODEL),
                         lambda i, p=p: ((layer * MOD_ROWS + mod_row_of_tile(i)) * 3 + p, 0, 0))
            for p in parts]


MOD_SHIFT, MOD_SCALE, MOD_GATE = 0, 1, 2


def _modnorm_call(x2d, mods, layer, mod_row_of_tile, norm_g, tm):
    t = x2d.shape[0]
    return pl.pallas_call(
        _modnorm_kernel,
        grid=(t // tm,),
        in_specs=[pl.BlockSpec((tm, D_MODEL), lambda i: (i, 0))]
        + _mod_specs(layer, mod_row_of_tile, (MOD_SHIFT, MOD_SCALE))
        + [pl.BlockSpec((1, 1, D_MODEL), lambda i: (layer, 0, 0))],
        out_specs=pl.BlockSpec((tm, D_MODEL), lambda i: (i, 0)),
        out_shape=jax.ShapeDtypeStruct((t, D_MODEL), BF16),
        compiler_params=pltpu.CompilerParams(
            dimension_semantics=("arbitrary",), vmem_limit_bytes=VMEM_LIMIT),
        name="modnorm",
    )(x2d, mods, mods, norm_g)


def _pack_bf16_rows(x):
    return pltpu.bitcast(x.astype(BF16), jnp.uint32)


def _in_proj_kernel(step_lo, names, sub, n_casts, h_ref, w_ref, qg_ref, kg_ref, hm_ref,
                    c_ref, s1_ref, s2_ref, *rest):
    cast_in, out_refs = rest[:n_casts], rest[n_casts:len(rest) - n_casts]
    cast_out = rest[len(rest) - n_casts:]
    out = dict(zip(names, out_refs))
    step = pl.program_id(1) + step_lo
    row_tiles = [slice(r, r + sub) for r in range(0, h_ref.shape[0], sub)]

    def tiles():
        for src, dst in zip(cast_in, cast_out):
            dst[...] = _pack_bf16_rows(src[...])
        w = w_ref[0].astype(BF16)
        for rows in row_tiles:
            yield rows, jnp.dot(h_ref[rows, :], w, preferred_element_type=F32)

    def head_tiles(g_ref):
        for rows, acc in tiles():
            yield rows, _head_norm_rope(acc, hm_ref[...], g_ref[0], c_ref[rows, :],
                                        s1_ref[rows, :], s2_ref[rows, :])

    if "qt" in out:
        @pl.when(step < K_TILE)
        def _():
            for rows, qh in head_tiles(qg_ref):
                out["qt"][:, rows] = qh.T.astype(BF16)

    if "k" in out:
        @pl.when(step == K_TILE)
        def _():
            for rows, kh in head_tiles(kg_ref):
                out["k"][rows, :] = kh.astype(BF16)

    if "vt" in out:
        @pl.when(step == V_TILE)
        def _():
            for rows, acc in tiles():
                out["vt"][:, rows] = acc.T.astype(BF16)

    if "rest" in out:
        @pl.when(step > V_TILE)
        def _():
            for rows, acc in tiles():
                out["rest"][rows, :] = acc.astype(BF16)


CAST_STEPS = 64


def _in_proj(h, w_in, layer, q_g, k_g, tables, step_lo, step_hi, group, sub, casts=()):
    t = h.shape[0]
    n_steps = step_hi - step_lo
    assert not casts or (t // group) * n_steps >= CAST_STEPS

    def cast_spec(rows):
        return pl.BlockSpec((rows // CAST_STEPS, D_MODEL),
                            lambda i, j: (jnp.minimum(i * n_steps + j, CAST_STEPS - 1), 0))

    cast_in_specs = [cast_spec(a.shape[0]) for a in casts]
    cast_out_specs = [cast_spec(a.shape[0] // 2) for a in casts]
    cast_shapes = [jax.ShapeDtypeStruct((a.shape[0] // 2, a.shape[1]), jnp.uint32)
                   for a in casts]
    c_tab, s1_tab, s2_tab = tables
    n_rest = REST_W // COL_TILE
    heads_per_tile = COL_TILE // HEAD_DIM
    head_mean = jnp.asarray(
        np.kron(np.eye(heads_per_tile), np.full((HEAD_DIM, HEAD_DIM), 1.0 / HEAD_DIM)), BF16)
    specs = {
        "qt": (pl.BlockSpec((COL_TILE, group),
                            lambda i, j: (jnp.clip(j + step_lo, 0, K_TILE - 1), i)),
               jax.ShapeDtypeStruct((ATT_W, t), BF16), step_lo < K_TILE),
        "k": (pl.BlockSpec((group, COL_TILE), lambda i, j: (i, 0)),
              jax.ShapeDtypeStruct((t, KV_W), BF16), step_lo <= K_TILE < step_hi),
        "vt": (pl.BlockSpec((COL_TILE, group), lambda i, j: (0, i)),
               jax.ShapeDtypeStruct((KV_W, t), BF16), step_lo <= V_TILE < step_hi),
        "rest": (pl.BlockSpec((group, COL_TILE),
                              lambda i, j: (i, jnp.clip(j + step_lo - V_TILE - 1, 0, n_rest - 1))),
                 jax.ShapeDtypeStruct((t, REST_W), BF16), step_hi > V_TILE + 1),
    }
    names = tuple(n for n, (_, _, present) in specs.items() if present)
    tab_spec = pl.BlockSpec((group, HEAD_DIM), lambda i, j: (0, 0))
    gain_spec = pl.BlockSpec((1, 1, COL_TILE), lambda i, j: (layer, 0, 0))
    outs = pl.pallas_call(
        functools.partial(_in_proj_kernel, step_lo, names, sub, len(casts)),
        grid=(t // group, n_steps),
        in_specs=[
            pl.BlockSpec((group, D_MODEL), lambda i, j: (i, 0)),
            pl.BlockSpec((1, D_MODEL, COL_TILE),
                         lambda i, j: (layer, 0, _weight_tile(j + step_lo))),
            gain_spec, gain_spec,
            pl.BlockSpec((COL_TILE, COL_TILE), lambda i, j: (0, 0)),
            tab_spec, tab_spec, tab_spec,
        ] + cast_in_specs,
        out_specs=[specs[n][0] for n in names] + cast_out_specs,
        out_shape=[specs[n][1] for n in names] + cast_shapes,
        compiler_params=pltpu.CompilerParams(
            dimension_semantics=("arbitrary", "arbitrary"), vmem_limit_bytes=VMEM_LIMIT),
        name="in_proj",
    )(h, w_in, q_g, k_g, head_mean, c_tab, s1_tab, s2_tab, *casts)
    result = dict(zip(names, outs))
    if casts:
        result["casts"] = outs[len(names):]
    return result


ONES_ROWS = 16


def _col_reduce8(x, op):
    rows, w = x.shape
    return op(x.reshape(rows // 8, 8, w), axis=0)


def _attn_kernel(chunks, tq, q_ref, *rest):
    n_sets = 1 + max(c[0] for c in chunks)
    k_refs, vt_refs = rest[:n_sets], rest[n_sets:2 * n_sets]
    o_ref, s_scr, m_scr = rest[2 * n_sets:]
    width = Q_PER_KV * tq

    @pl.when(pl.program_id(0) == 0)
    def _():
        s_scr[...] = jnp.zeros_like(s_scr)
        m_scr[...] = jnp.zeros_like(m_scr)

    q_t = jnp.concatenate(
        [q_ref[i * HEAD_DIM:(i + 1) * HEAD_DIM, :] for i in range(Q_PER_KV)], axis=1)
    m_prev = m_scr[...]
    m_run = jnp.full((8, width), -jnp.inf, F32)
    acc = jnp.zeros((HEAD_DIM + ONES_ROWS, width), F32)
    for t, r0, rows, s0 in chunks:
        p = jnp.exp2(s_scr[s0:s0 + rows, :] - m_prev).astype(BF16)
        vt_ones = jnp.concatenate(
            [vt_refs[t][:, r0:r0 + rows], jnp.ones((ONES_ROWS, rows), BF16)], axis=0)
        acc = acc + jnp.dot(vt_ones, p, preferred_element_type=F32)
        s_new = jnp.dot(k_refs[t][r0:r0 + rows, :], q_t, preferred_element_type=F32)
        s_scr[s0:s0 + rows, :] = s_new
        m_run = jnp.maximum(m_run, _col_reduce8(s_new, jnp.max))
    m_scr[...] = m_run.max(axis=0, keepdims=True)
    o_t = acc[:HEAD_DIM] / acc[HEAD_DIM:HEAD_DIM + 1]
    for i in range(Q_PER_KV):
        o_ref[:, i * HEAD_DIM:(i + 1) * HEAD_DIM] = o_t[:, i * tq:(i + 1) * tq].T.astype(BF16)


def _attention(qt, kv_sets, batch, tq, key_chunk):
    t = qt.shape[1]
    nq = t // batch // tq
    n_tiles = batch * N_KV_HEADS * nq
    gw = Q_PER_KV * HEAD_DIM

    def tile(n):
        return n // (N_KV_HEADS * nq), (n // nq) % N_KV_HEADS, n % nq

    def score_tile(n):
        return tile(jnp.minimum(n, n_tiles - 1))

    def out_tile(n):
        return tile(jnp.maximum(n - 1, 0))

    def q_map(n):
        b, g, i = score_tile(n)
        return g, b * nq + i

    def o_map(n):
        b, g, i = out_tile(n)
        return b * nq + i, g

    def k_map(n):
        b, g, _ = score_tile(n)
        return b, g

    def vt_map(n):
        b, g, _ = out_tile(n)
        return g, b

    chunks, k_specs, vt_specs, s_rows = [], [], [], 0
    for s, (k, _) in enumerate(kv_sets):
        m_t = k.shape[0] // batch
        step = min(key_chunk, m_t)
        for r0 in range(0, m_t, step):
            chunks.append((s, r0, step, s_rows + r0))
        s_rows += m_t
        k_specs.append(pl.BlockSpec((m_t, HEAD_DIM), k_map))
        vt_specs.append(pl.BlockSpec((HEAD_DIM, m_t), vt_map))
    return pl.pallas_call(
        functools.partial(_attn_kernel, tuple(chunks), tq),
        grid=(n_tiles + 1,),
        in_specs=[pl.BlockSpec((gw, tq), q_map)] + k_specs + vt_specs,
        out_specs=pl.BlockSpec((tq, gw), o_map),
        out_shape=jax.ShapeDtypeStruct((t, ATT_W), BF16),
        scratch_shapes=[pltpu.VMEM((s_rows, Q_PER_KV * tq), F32),
                        pltpu.VMEM((1, Q_PER_KV * tq), F32)],
        compiler_params=pltpu.CompilerParams(
            dimension_semantics=("arbitrary",), vmem_limit_bytes=VMEM_LIMIT),
        name="attn",
    )(qt, *[k for k, _ in kv_sets], *[vt for _, vt in kv_sets])


DFT_PAD_ROWS = 16
FLIP_BLOCK = 256


def _dft_tables(n):
    h = n // 2
    k = np.arange(h + 1, dtype=np.int64)[:, None]
    j = np.arange(h, dtype=np.int64)[None, :]
    ang = 2.0 * np.pi * ((k * j) % n).astype(np.float64) / n
    c_half = np.zeros((h + DFT_PAD_ROWS, h))
    c_half[:h + 1] = np.cos(ang) / np.sqrt(n)
    s_half = np.sin(ang[:h]) / np.sqrt(n)
    blk = min(FLIP_BLOCK, h)
    flip = np.zeros((blk, blk))
    flip[np.arange(1, blk), blk - np.arange(1, blk)] = 1.0
    ch = np.arange(FNET_GROUP, dtype=np.int64)
    ang_c = 2.0 * np.pi * ((ch[:, None] * ch[None, :]) % FNET_GROUP) / FNET_GROUP
    cc, sc = np.cos(ang_c) / np.sqrt(FNET_GROUP), np.sin(ang_c) / np.sqrt(FNET_GROUP)
    return tuple(a.astype(np.float32) for a in (c_half, s_half, flip, cc, sc))


def _flip_rows(x, flip):
    h, w = x.shape
    blk = flip.shape[0]
    nb = h // blk
    is_row0 = lax.broadcasted_iota(jnp.int32, (blk, w), 0) == 0
    blocks = [x[b * blk:(b + 1) * blk, :] for b in range(nb)]
    out = []
    for b in range(nb):
        inner = jnp.dot(flip, blocks[nb - b - 1], preferred_element_type=F32)
        first = blocks[(nb - b) % nb][0:1, :].astype(F32)
        out.append(jnp.where(is_row0, first, inner))
    return jnp.concatenate(out, axis=0)


def _fourier_kernel(n, u_ref, ch_ref, sh_ref, flip_ref, cc_ref, sc_ref, o_ref, a_scr, b_scr):
    h = n // 2
    is_row0 = lax.broadcasted_iota(jnp.int32, (h, FNET_W), 0) == 0
    u_top = u_ref[0:h, :].astype(F32)
    v = _flip_rows(u_ref[h:n, :], flip_ref[...])
    even = (u_top + jnp.where(is_row0, 0.0, v)).astype(BF16)
    odd = jnp.where(is_row0, 0.0, u_top - v).astype(BF16)
    pad_row0 = lax.broadcasted_iota(jnp.int32, (DFT_PAD_ROWS, FNET_W), 0) == 0
    mid = jnp.where(pad_row0, v[0:1, :], 0.0).astype(BF16)
    a_mid = []
    for g in range(N_FNET_GROUPS):
        cols = slice(g * FNET_GROUP, (g + 1) * FNET_GROUP)
        a_scr[:, cols] = jnp.dot(even[:, cols], cc_ref[...],
                                 preferred_element_type=F32).astype(BF16)
        b_scr[:, cols] = jnp.dot(odd[:, cols], sc_ref[...],
                                 preferred_element_type=F32).astype(BF16)
        a_mid.append(jnp.dot(mid[:, cols], cc_ref[...], preferred_element_type=F32)[0:1, :])
    a_mid = jnp.concatenate(a_mid, axis=1) * (n ** -0.5)
    p = jnp.dot(ch_ref[...], a_scr[...], preferred_element_type=F32)
    k_is_even = (lax.broadcasted_iota(jnp.int32, p.shape, 0) & 1) == 0
    p = p + jnp.where(k_is_even, a_mid, -a_mid)
    q = jnp.dot(sh_ref[...], b_scr[...], preferred_element_type=F32)
    o_ref[0:h, :] = (p[:h] - q).astype(BF16)
    z = jnp.where(is_row0, p[h:h + 1], p[:h] + q).astype(BF16)
    o_ref[h:n, :] = _flip_rows(z, flip_ref[...]).astype(BF16)


def _fourier(rest, batch):
    t = rest.shape[0]
    n = t // batch
    tables = [jnp.asarray(a).astype(BF16) for a in _dft_tables(n)]

    def whole(a):
        return pl.BlockSpec(a.shape, lambda b: (0, 0), pipeline_mode=pl.Buffered(1))

    return pl.pallas_call(
        functools.partial(_fourier_kernel, n),
        grid=(batch,),
        in_specs=[pl.BlockSpec((n, FNET_W), lambda b: (b, REST_UB_BLOCK))]
        + [whole(a) for a in tables],
        out_specs=pl.BlockSpec((n, FNET_W), lambda b: (b, 0)),
        out_shape=jax.ShapeDtypeStruct((t, FNET_W), BF16),
        scratch_shapes=[pltpu.VMEM((n // 2, FNET_W), BF16), pltpu.VMEM((n // 2, FNET_W), BF16)],
        compiler_params=pltpu.CompilerParams(
            dimension_semantics=("arbitrary",), vmem_limit_bytes=VMEM_LIMIT),
        name="fourier",
    )(rest, *tables)


def _unpack_weight(w_ref):
    return pltpu.bitcast(w_ref[0], BF16)


def _mix_kernel(attn_ref, za_ref, fm_ref, zb_ref, ga_ref, gb_ref, wpa_ref, wpb_ref, mix_ref):
    ta = (attn_ref[...].astype(F32) * _silu(za_ref[...].astype(F32))).astype(BF16)
    ya = jnp.dot(ta, _unpack_weight(wpa_ref), preferred_element_type=F32)
    tb = (fm_ref[...].astype(F32) * _silu(zb_ref[...].astype(F32))).astype(BF16)
    yb = jnp.dot(tb, _unpack_weight(wpb_ref), preferred_element_type=F32)
    mix_ref[...] = (jax.nn.sigmoid(ga_ref[...].astype(F32)) * ya
                    + jax.nn.sigmoid(gb_ref[...].astype(F32)) * yb).astype(BF16)


def _out_kernel(final, mix_ref, x_ref, gate_ref, wo_ref, *rest):
    y = jnp.dot(mix_ref[...], _unpack_weight(wo_ref), preferred_element_type=F32)
    out = x_ref[...] + gate_ref[0] * y
    if final:
        fg_ref, o_ref = rest
        ms = jnp.mean(out * out, axis=-1, keepdims=True)
        o_ref[...] = out * lax.rsqrt(ms + EPS) * fg_ref[...]
    else:
        shift_ref, scale_ref, g_ref, o_ref, h_ref = rest
        o_ref[...] = out
        h_ref[...] = _modnorm(out, g_ref[0], scale_ref[0], shift_ref[0]).astype(BF16)


def _merge(rest, attn, fm, x2d, mods, layer, mod_row_of_tile, w_pa, w_pb, w_o, norm_g, final_g,
           tm):
    t = x2d.shape[0]
    final = layer == DEPTH - 1
    tm2 = 2 * tm

    def rows(width, block=0):
        return pl.BlockSpec((tm2, width), lambda i: (i, block))

    def weight(k):
        return pl.BlockSpec((1, k // 2, D_MODEL), lambda i: (layer, 0, 0),
                            pipeline_mode=pl.Buffered(1))

    def mod_row(i):
        return mod_row_of_tile(2 * i)

    params = pltpu.CompilerParams(
        dimension_semantics=("arbitrary",), vmem_limit_bytes=VMEM_LIMIT)
    mix = pl.pallas_call(
        _mix_kernel,
        grid=(t // tm2,),
        in_specs=[
            rows(ATT_W), rows(ATT_W, REST_ZA_BLOCK), rows(FNET_W), rows(FNET_W, REST_ZB_BLOCK),
            rows(D_MODEL, REST_GA_BLOCK), rows(D_MODEL, REST_GB_BLOCK),
            weight(ATT_W), weight(FNET_W)],
        out_specs=rows(D_MODEL),
        out_shape=jax.ShapeDtypeStruct((t, D_MODEL), BF16),
        compiler_params=params,
        name="mix",
    )(attn, rest, fm, rest, rest, rest, w_pa, w_pb)

    in_specs = [rows(D_MODEL), rows(D_MODEL)] + _mod_specs(layer, mod_row, (MOD_GATE,)) + [
        weight(D_MODEL)]
    args = [mix, x2d, mods, w_o]
    stream = jax.ShapeDtypeStruct((t, D_MODEL), F32)
    if final:
        in_specs += [pl.BlockSpec((1, D_MODEL), lambda i: (0, 0))]
        args += [final_g]
        out_specs, out_shape = rows(D_MODEL), stream
    else:
        in_specs += _mod_specs(layer + 1, mod_row, (MOD_SHIFT, MOD_SCALE))
        in_specs += [pl.BlockSpec((1, 1, D_MODEL), lambda i: (layer + 1, 0, 0))]
        args += [mods, mods, norm_g]
        out_specs = [rows(D_MODEL), rows(D_MODEL)]
        out_shape = [stream, jax.ShapeDtypeStruct((t, D_MODEL), BF16)]
    return pl.pallas_call(
        functools.partial(_out_kernel, final),
        grid=(t // tm2,),
        in_specs=in_specs,
        out_specs=out_specs,
        out_shape=out_shape,
        compiler_params=params,
        name="merge",
    )(*args)


def _rope_tables(n_tokens):
    t = np.arange(n_tokens)
    pos = np.stack([t // GRID_W, t % GRID_W], axis=1).astype(np.float32)
    inv = (np.float32(ROPE_THETA) ** (-np.arange(ROPE_NFREQ, dtype=np.float32) / ROPE_NFREQ))
    ang = (pos[:, :, None] * inv[None, None, :]).astype(np.float32).astype(np.float64)
    cos, sin = np.cos(ang), np.sin(ang)
    zero = np.zeros_like(sin)
    c = np.stack([cos, cos], axis=2).reshape(n_tokens, HEAD_DIM)
    s1 = np.stack([-sin, zero], axis=2).reshape(n_tokens, HEAD_DIM)
    s2 = np.stack([zero, sin], axis=2).reshape(n_tokens, HEAD_DIM)
    return tuple(jnp.asarray(a.astype(np.float32)) for a in (c, s1, s2))


def _identity_tables(n_tokens):
    one = jnp.ones((n_tokens, HEAD_DIM), F32)
    zero = jnp.zeros((n_tokens, HEAD_DIM), F32)
    return one, zero, zero


def kernel(x, c, ctx, c_ctx, w_ada, b_ada, norm_g, w_in, q_norm_g, k_norm_g,
           w_proj_a, w_proj_b, w_out, final_g):
    batch, n_lat, d = x.shape
    n_ctx = ctx.shape[1]
    tm_norm, tm_merge, sub = 1024, 256, 512
    ctx_row = batch

    cvec = jnp.concatenate(
        [c, c_ctx[None], jnp.zeros((MOD_ROWS - batch - 1, d), F32)], axis=0)
    mods = _ada(cvec, w_ada, b_ada).reshape(DEPTH * MOD_ROWS * 3, 1, d)

    rope = _rope_tables(n_lat)
    no_rope = _identity_tables(batch * n_ctx)
    heads_per_tile = COL_TILE // HEAD_DIM
    qg = jnp.tile(q_norm_g * Q_SCALE, (1, heads_per_tile)).reshape(DEPTH, 1, COL_TILE)
    kg = jnp.tile(k_norm_g, (1, heads_per_tile)).reshape(DEPTH, 1, COL_TILE)
    merge_w = (w_proj_a, w_proj_b, w_out)
    norm_g3 = norm_g.reshape(DEPTH, 1, d)
    fg = final_g.reshape(1, d)

    def lat_row(tm):
        return lambda i: i // (n_lat // tm)

    def ctx_mod_row(i):
        return ctx_row

    xs = x.reshape(batch * n_lat, d)
    cs = ctx.reshape(batch * n_ctx, d)
    h = _modnorm_call(xs, mods, 0, lat_row(tm_norm), norm_g3, tm_norm)
    hc = _modnorm_call(cs, mods, 0, ctx_mod_row, norm_g3, tm_norm // 2)
    for l in range(DEPTH):
        last = l == DEPTH - 1
        lo, hi = (K_TILE, V_TILE + 1) if last else (0, N_COL_TILES)
        pc = _in_proj(hc, w_in, l, qg, kg, no_rope, lo, hi, batch * n_ctx, sub)
        casts = tuple(w.reshape(-1, d) for w in merge_w) if l == 0 else ()
        p = _in_proj(h, w_in, l, qg, kg, rope, 0, N_COL_TILES, n_lat, sub, casts)
        if l == 0:
            w_pa, w_pb, w_o = (c.reshape(DEPTH, -1, d) for c in p["casts"])

        attn = _attention(p["qt"], [(p["k"], p["vt"]), (pc["k"], pc["vt"])], batch, 256, 256)
        fm = _fourier(p["rest"], batch)
        merged = _merge(p["rest"], attn, fm, xs, mods, l, lat_row(tm_merge), w_pa, w_pb, w_o,
                        norm_g3, fg, tm_merge)
        if last:
            xs = merged
        else:
            xs, h = merged
            attn_c = _attention(pc["qt"], [(pc["k"], pc["vt"])], batch, n_ctx, n_ctx)
            fm_c = _fourier(pc["rest"], batch)
            cs, hc = _merge(pc["rest"], attn_c, fm_c, cs, mods, l, ctx_mod_row, w_pa, w_pb, w_o,
                            norm_g3, fg, tm_merge)
    return xs.reshape(batch, n_lat, d)
```
